```python
import math
import jax, jax.numpy as jnp
from jax import lax
import numpy as np

D_MODEL = 1024
BATCH = 16
SEQ = 2048
DEPTH = 1

D_MIX = D_MODEL
D_RWKV = D_MIX // 2
D_MLSTM = D_MIX - D_RWKV
RWKV_HEAD = 64
RWKV_HEADS = D_RWKV // RWKV_HEAD
MLSTM_HEADS = 4
MLSTM_HEAD = D_MLSTM // MLSTM_HEADS
W_LORA = 64
A_LORA = 64
G_LORA = 128
D_RWKV_IN = 3 * D_RWKV + W_LORA + A_LORA + G_LORA
D_MLSTM_IN = 4 * D_MLSTM + 2 * MLSTM_HEADS
D_IN = D_RWKV_IN + D_MLSTM_IN
RWKV_SPLITS = (D_RWKV, 2 * D_RWKV, 3 * D_RWKV, 3 * D_RWKV + W_LORA, 3 * D_RWKV + W_LORA + A_LORA)
QK_CONV = 4
MLSTM_CHUNK = 64
D_FF = 2816
FF_CONV = 3
NORM_EPS = 1e-6
LNX_EPS = 64e-5
MHN_EPS = 1e-6

kernel_name = 'hybrid_rwkv7_mlstm_convffn_block'


def rms_norm(x, g):
    xf = x.astype(jnp.float32)
    y = xf * lax.rsqrt(jnp.mean(xf * xf, axis=-1, keepdims=True) + NORM_EPS)
    return (y * g.astype(jnp.float32)).astype(x.dtype)


def head_layer_norm(y, eps):
    mu = jnp.mean(y, axis=-1, keepdims=True)
    var = jnp.mean(jnp.square(y - mu), axis=-1, keepdims=True)
    return (y - mu) * lax.rsqrt(var + eps)


def token_shift(t):
    return jnp.pad(t[:, :-1], ((0, 0), (1, 0), (0, 0)))


def causal_dwconv(t, w, b):
    K, C = w.shape
    y = lax.conv_general_dilated(t, w[:, None, :].astype(t.dtype), window_strides=(1,),
                                 padding=[(K - 1, 0)], dimension_numbers=('NWC', 'WIO', 'NWC'),
                                 feature_group_count=C)
    return y + b


def modulate(h, shift, scale):
    return h * (1 + scale[:, None, :]) + shift[:, None, :]


def rwkv7_step(S, inp):
    r_t, w_t, k_t, v_t, a_t, b_t = inp
    sa = jnp.einsum('bhvk,bhk->bhv', S, a_t)
    S = S * w_t[:, :, None, :] + sa[..., None] * b_t[:, :, None, :] + v_t[..., None] * k_t[:, :, None, :]
    return S, jnp.einsum('bhvk,bhk->bhv', S, r_t)


def rwkv7_time_mix(p, mu, w_w2, w0, w_a2, a0, w_g2, k_k, k_a, r_k, lnx_w, lnx_b):
    B, T, _ = p.shape
    H, N = RWKV_HEADS, RWKV_HEAD
    p = p + (token_shift(p) - p) * mu
    r, k, v, xw, xa, xg = jnp.split(p, RWKV_SPLITS, axis=-1)
    w = -jax.nn.softplus(-(w0 + jnp.tanh(xw) @ w_w2)) - 0.5
    a = jax.nn.sigmoid(a0 + xa @ w_a2)
    g = jax.nn.sigmoid(xg) @ w_g2
    heads = lambda t: t.reshape(B, T, H, N).astype(jnp.float32)
    kk = heads(k * k_k)
    kk = kk / jnp.maximum(jnp.sqrt(jnp.sum(kk * kk, axis=-1, keepdims=True)), 1e-12)
    k = k * (1 + (a - 1) * k_a)
    r_h, k_h, v_h, a_h = heads(r), heads(k), heads(v), heads(a)
    decay = jnp.exp(-jnp.exp(heads(w)))
    tm = lambda t: jnp.swapaxes(t, 0, 1)
    S0 = jnp.zeros((B, H, N, N), jnp.float32)
    _, y = lax.scan(rwkv7_step, S0, (tm(r_h), tm(decay), tm(k_h), tm(v_h), tm(-kk), tm(kk * a_h)))
    y = tm(y)
    y = head_layer_norm(y, LNX_EPS) * lnx_w.reshape(H, N).astype(jnp.float32) + lnx_b.reshape(H, N).astype(jnp.float32)
    y = y + jnp.sum(r_h * k_h * r_k.astype(jnp.float32), axis=-1, keepdims=True) * v_h
    return y.reshape(B, T, D_RWKV).astype(p.dtype) * g


def mlstm_chunk(carry, xs):
    C, n, m = carry
    q, k, v, ig, lf = xs
    causal = jnp.tril(jnp.ones((MLSTM_CHUNK, MLSTM_CHUNK), dtype=bool))
    b = jnp.cumsum(lf, axis=-1)
    D = jnp.where(causal, b[..., :, None] - b[..., None, :] + ig[..., None, :], -jnp.inf)
    inter = b + m[..., None]
    m_t = jnp.maximum(inter, jnp.max(D, axis=-1))
    Dw = jnp.exp(D - m_t[..., None])
    iw = jnp.exp(inter - m_t)
    s = jnp.einsum('bhtd,bhsd->bhts', q, k) * Dw
    num = jnp.einsum('bhts,bhse->bhte', s, v) + iw[..., None] * jnp.einsum('bhtd,bhde->bhte', q, C)
    den = jnp.sum(s, axis=-1) + iw * jnp.einsum('bhtd,bhd->bht', q, n)
    h = num / jnp.maximum(jnp.abs(den), jnp.exp(-m_t))[..., None]
    bL = b[..., -1]
    gl = bL[..., None] - b + ig
    m_new = jnp.maximum(bL + m, jnp.max(gl, axis=-1))
    ws = jnp.exp(gl - m_new[..., None])
    dec = jnp.exp(bL + m - m_new)
    C_new = dec[..., None, None] * C + jnp.einsum('bhs,bhsd,bhse->bhde', ws, k, v)
    n_new = dec[..., None] * n + jnp.einsum('bhs,bhsd->bhd', ws, k)
    return (C_new, n_new, m_new), h


def mlstm_mix(p, conv_qk_w, conv_qk_b, i_bias, f_bias, mhn_w):
    B, T, _ = p.shape
    H, N, L = MLSTM_HEADS, MLSTM_HEAD, MLSTM_CHUNK
    nc = T // L
    qk = jax.nn.silu(causal_dwconv(p[..., :2 * D_MLSTM], conv_qk_w, conv_qk_b))
    q, k = jnp.split(qk, 2, axis=-1)
    v = p[..., 2 * D_MLSTM:3 * D_MLSTM]
    o = p[..., 3 * D_MLSTM:4 * D_MLSTM]
    ig = (p[..., 4 * D_MLSTM:4 * D_MLSTM + H] + i_bias).astype(jnp.float32)
    lf = jax.nn.log_sigmoid((p[..., 4 * D_MLSTM + H:] + f_bias).astype(jnp.float32))
    chunks = lambda t: t.reshape(B, nc, L, H, N).transpose(1, 0, 3, 2, 4).astype(jnp.float32)
    gchunks = lambda t: t.reshape(B, nc, L, H).transpose(1, 0, 3, 2)
    carry0 = (jnp.zeros((B, H, N, N), jnp.float32), jnp.zeros((B, H, N), jnp.float32), jnp.zeros((B, H), jnp.float32))
    _, h = lax.scan(mlstm_chunk, carry0, (chunks(q), chunks(k) * (N ** -0.5), chunks(v), gchunks(ig), gchunks(lf)))
    h = h.transpose(1, 0, 3, 2, 4).reshape(B, T, H, N)
    h = head_layer_norm(h, MHN_EPS) * mhn_w.reshape(H, N).astype(jnp.float32)
    return h.reshape(B, T, D_MLSTM).astype(p.dtype) * jax.nn.sigmoid(o)


def hybrid_layer(x, c, w_ada, b_ada, norm1_g, norm2_g, w_in, mu_rwkv, w_w2, w0, w_a2, a0, w_g2,
                 k_k, k_a, r_k, lnx_w, lnx_b, conv_qk_w, conv_qk_b, i_bias, f_bias, mhn_w,
                 w_out, w_up, conv_ff_w, conv_ff_b, w_down):
    mod = jax.nn.silu(c) @ w_ada + b_ada
    sh1, sc1, g1, sh2, sc2, g2 = jnp.split(mod, 6, axis=-1)
    h = modulate(rms_norm(x, norm1_g), sh1, sc1)
    p = h @ w_in
    y_rwkv = rwkv7_time_mix(p[..., :D_RWKV_IN], mu_rwkv, w_w2, w0, w_a2, a0, w_g2, k_k, k_a, r_k, lnx_w, lnx_b)
    y_mlstm = mlstm_mix(p[..., D_RWKV_IN:], conv_qk_w, conv_qk_b, i_bias, f_bias, mhn_w)
    y = jnp.concatenate([y_rwkv, y_mlstm], axis=-1) @ w_out
    x = x + g1[:, None, :] * y
    h = modulate(rms_norm(x, norm2_g), sh2, sc2)
    u = causal_dwconv(h @ w_up, conv_ff_w, conv_ff_b)
    u_act, u_lin = jnp.split(u, 2, axis=-1)
    x = x + g2[:, None, :] * ((jax.nn.silu(u_act) * u_lin) @ w_down)
    return x


def setup_inputs(seed: int = 0) -> dict:
    key = jax.random.key(seed)
    ks = jax.random.split(key, 32)
    nrm = lambda k, shape, scale: jax.random.normal(k, shape, jnp.float32) * scale
    Lr = DEPTH
    ramp = jnp.arange(D_RWKV, dtype=jnp.float32) / (D_RWKV - 1)
    w0_base = -7.0 + 5.0 * ramp ** 0.85 + 0.5
    f_base = jnp.linspace(3.0, 6.0, MLSTM_HEADS, dtype=jnp.float32)
    return {
        'x': nrm(ks[0], (BATCH, SEQ, D_MODEL), 1.0),
        'c': nrm(ks[1], (BATCH, D_MODEL), 1.0),
        'w_ada': nrm(ks[2], (Lr, D_MODEL, 6 * D_MODEL), D_MODEL ** -0.5),
        'b_ada': nrm(ks[3], (Lr, 6 * D_MODEL), 0.01),
        'norm1_g': 1.0 + nrm(ks[4], (Lr, D_MODEL), 0.01),
        'norm2_g': 1.0 + nrm(ks[5], (Lr, D_MODEL), 0.01),
        'normf_g': 1.0 + nrm(ks[6], (D_MODEL,), 0.01),
        'w_in': nrm(ks[7], (Lr, D_MODEL, D_IN), D_MODEL ** -0.5),
        'mu_rwkv': jax.random.uniform(ks[8], (Lr, D_RWKV_IN), jnp.float32),
        'w_w2': nrm(ks[9], (Lr, W_LORA, D_RWKV), 0.1 * W_LORA ** -0.5),
        'w0': w0_base + nrm(ks[10], (Lr, D_RWKV), 0.01),
        'w_a2': nrm(ks[11], (Lr, A_LORA, D_RWKV), 0.1 * A_LORA ** -0.5),
        'a0': nrm(ks[12], (Lr, D_RWKV), 0.1),
        'w_g2': nrm(ks[13], (Lr, G_LORA, D_RWKV), G_LORA ** -0.5),
        'k_k': 0.85 + nrm(ks[14], (Lr, D_RWKV), 0.01),
        'k_a': 1.0 + nrm(ks[15], (Lr, D_RWKV), 0.01),
        'r_k': nrm(ks[16], (Lr, RWKV_HEADS, RWKV_HEAD), 0.1),
        'lnx_w': 1.0 + nrm(ks[17], (Lr, D_RWKV), 0.01),
        'lnx_b': nrm(ks[18], (Lr, D_RWKV), 0.01),
        'conv_qk_w': nrm(ks[19], (Lr, QK_CONV, 2 * D_MLSTM), QK_CONV ** -0.5),
        'conv_qk_b': nrm(ks[20], (Lr, 2 * D_MLSTM), 0.01),
        'i_bias': nrm(ks[21], (Lr, MLSTM_HEADS), 0.1),
        'f_bias': f_base + nrm(ks[22], (Lr, MLSTM_HEADS), 0.1),
        'mhn_w': 1.0 + nrm(ks[23], (Lr, D_MLSTM), 0.01),
        'w_out': nrm(ks[24], (Lr, D_MIX, D_MODEL), D_MIX ** -0.5),
        'w_up': nrm(ks[25], (Lr, D_MODEL, 2 * D_FF), D_MODEL ** -0.5),
        'conv_ff_w': nrm(ks[26], (Lr, FF_CONV, 2 * D_FF), FF_CONV ** -0.5),
        'conv_ff_b': nrm(ks[27], (Lr, 2 * D_FF), 0.01),
        'w_down': nrm(ks[28], (Lr, D_FF, D_MODEL), D_FF ** -0.5),
    }


def reference(x, c, w_ada, b_ada, norm1_g, norm2_g, normf_g, w_in, mu_rwkv, w_w2, w0, w_a2, a0,
              w_g2, k_k, k_a, r_k, lnx_w, lnx_b, conv_qk_w, conv_qk_b, i_bias, f_bias, mhn_w,
              w_out, w_up, conv_ff_w, conv_ff_b, w_down):
    for l in range(DEPTH):
        x = hybrid_layer(x, c, w_ada[l], b_ada[l], norm1_g[l], norm2_g[l], w_in[l], mu_rwkv[l],
                         w_w2[l], w0[l], w_a2[l], a0[l], w_g2[l], k_k[l], k_a[l], r_k[l],
                         lnx_w[l], lnx_b[l], conv_qk_w[l], conv_qk_b[l], i_bias[l], f_bias[l],
                         mhn_w[l], w_out[l], w_up[l], conv_ff_w[l], conv_ff_b[l], w_down[l])
    return rms_norm(x, normf_g)
```

```python
import functools

import jax
import jax.numpy as jnp
from jax import lax
from jax.experimental import pallas as pl
from jax.experimental.pallas import tpu as pltpu

D_MODEL = 1024
D_RWKV = 512
RWKV_HEAD = 64
RWKV_HEADS = 8
RWKV_PAIRS = RWKV_HEADS // 2
W_LORA = 64
A_LORA = 64
G_LORA = 128
D_MLSTM = 512
MLSTM_HEADS = 4
MLSTM_HEAD = 128
QK_CONV = 4
MLSTM_CHUNK = 64
D_FF = 2816
FF_CONV = 3
NORM_EPS = 1e-6
LNX_EPS = 64e-5
MHN_EPS = 1e-6

LANES = 128
SUBLANES = 8
RWKV_CHUNK = 64
INV_BASE = 8
INV_BASE_SQUARINGS = 2
D_R = 3 * D_RWKV + 3 * LANES
D_M = 4 * D_MLSTM
D_P = D_R + D_M + LANES
N_GATES = 2 * MLSTM_HEADS

TM_IN = 512
TB_RWKV = 256
TB_MLSTM = 256
TM_FFN = 512
FF_TILE = 256
VMEM_LIMIT = 56 * 1024 * 1024

F32 = jnp.float32
BF16 = jnp.bfloat16


def _dot(a, b):
    return jnp.dot(a.astype(BF16), b.astype(BF16), preferred_element_type=F32)


def _dot_nt(a, b):
    return lax.dot_general(a.astype(BF16), b.astype(BF16), (((1,), (1,)), ((), ())),
                           preferred_element_type=F32)


def _dot_tn(a, b):
    return lax.dot_general(a.astype(BF16), b.astype(BF16), (((0,), (0,)), ((), ())),
                           preferred_element_type=F32)


def _dot3(a, b):
    a_hi = a.astype(BF16)
    a_lo = (a - a_hi.astype(F32)).astype(BF16)
    b_hi = b.astype(BF16)
    b_lo = (b - b_hi.astype(F32)).astype(BF16)
    d = lambda u, w: jnp.dot(u, w, preferred_element_type=F32)
    return d(a_hi, b_hi) + (d(a_hi, b_lo) + d(a_lo, b_hi))


def _split3(x):
    hi = x.astype(BF16)
    r1 = x - hi.astype(F32)
    mid = r1.astype(BF16)
    lo = (r1 - mid.astype(F32)).astype(BF16)
    return hi, mid, lo


def _dot_sel_l(sel, x):
    hi, mid, lo = _split3(x)
    d = lambda t: jnp.dot(sel, t, preferred_element_type=F32)
    return d(hi) + (d(mid) + d(lo))


def _dot_sel_r(x, sel):
    hi, mid, lo = _split3(x)
    d = lambda t: jnp.dot(t, sel, preferred_element_type=F32)
    return d(hi) + (d(mid) + d(lo))


def _log_sigmoid(z):
    return jnp.minimum(z, 0.0) - jnp.log1p(jnp.exp(-jnp.abs(z)))


def _silu(z):
    return z * jax.nn.sigmoid(z)


def _const_spec(shape):
    nd = len(shape)
    return pl.BlockSpec(shape, lambda *_: (0,) * nd, pipeline_mode=pl.Buffered(1))


def _ada_kernel(c_ref, w_ref, b_ref, o_ref):
    s = _silu(c_ref[...])
    o_ref[0] = jnp.dot(s, w_ref[...], preferred_element_type=F32,
                       precision=lax.Precision.HIGHEST) + b_ref[0]


def _ada(c, w_ada, b_ada):
    B = c.shape[0]
    return pl.pallas_call(
        _ada_kernel,
        grid=(6,),
        in_specs=[pl.BlockSpec((B, D_MODEL), lambda j: (0, 0)),
                  pl.BlockSpec((D_MODEL, D_MODEL), lambda j: (0, j)),
                  pl.BlockSpec((1, 1, D_MODEL), lambda j: (j, 0, 0))],
        out_specs=pl.BlockSpec((1, B, D_MODEL), lambda j: (j, 0, 0)),
        out_shape=jax.ShapeDtypeStruct((6, B, D_MODEL), F32),
        compiler_params=pltpu.CompilerParams(dimension_semantics=("arbitrary",),
                                             vmem_limit_bytes=VMEM_LIMIT),
        name="ada",
    )(c, w_ada, b_ada.reshape(6, 1, D_MODEL))


def _rms_mod(x, g, shift, scale):
    y = x * lax.rsqrt(jnp.mean(x * x, axis=-1, keepdims=True) + NORM_EPS) * g
    return y * (1.0 + scale) + shift


def _inproj_kernel(x_ref, mod_ref, g_ref, w_ref, wgt_ref, pr_ref, pm_ref, pg_ref, grow_ref):
    h = _rms_mod(x_ref[...], g_ref[...], mod_ref[0:1, :], mod_ref[1:2, :]).astype(BF16)
    pr_ref[...] = jnp.dot(h, w_ref[:, 0:D_R], preferred_element_type=F32)
    pm_ref[...] = jnp.dot(h, w_ref[:, D_R:D_R + D_M], preferred_element_type=F32)
    pg_ref[...] = jnp.dot(h, w_ref[:, D_R + D_M:D_P], preferred_element_type=F32)
    grow_ref[...] = lax.dot_general(wgt_ref[...], h, (((1,), (1,)), ((), ())),
                                    preferred_element_type=F32)


def _inproj(x, mod, norm_g, w_all, w_gt):
    B, T, _ = x.shape
    return pl.pallas_call(
        _inproj_kernel,
        grid=(B, T // TM_IN),
        in_specs=[pl.BlockSpec((None, TM_IN, D_MODEL), lambda b, t: (b, t, 0)),
                  pl.BlockSpec((None, 6, D_MODEL), lambda b, t: (b, 0, 0)),
                  _const_spec((1, D_MODEL)),
                  _const_spec((D_MODEL, D_P)),
                  _const_spec((N_GATES, D_MODEL))],
        out_specs=[pl.BlockSpec((None, TM_IN, D_R), lambda b, t: (b, t, 0)),
                   pl.BlockSpec((None, TM_IN, D_M), lambda b, t: (b, t, 0)),
                   pl.BlockSpec((None, TM_IN, LANES), lambda b, t: (b, t, 0)),
                   pl.BlockSpec((None, N_GATES, TM_IN), lambda b, t: (b, 0, t))],
        out_shape=[jax.ShapeDtypeStruct((B, T, D_R), F32),
                   jax.ShapeDtypeStruct((B, T, D_M), F32),
                   jax.ShapeDtypeStruct((B, T, LANES), F32),
                   jax.ShapeDtypeStruct((B, N_GATES, T), F32)],
        compiler_params=pltpu.CompilerParams(dimension_semantics=("arbitrary", "arbitrary"),
                                             vmem_limit_bytes=VMEM_LIMIT),
        name="inproj",
    )(x, mod, norm_g, w_all, w_gt)


def _rwkv_kernel(p_ref, mu_ref, ww2_ref, w0_ref, wa2_ref, a0_ref, wg2_ref, kk_ref, ka_ref, rk_ref,
                 lnw_ref, lnb_ref, seg_ref, tri_ref, ones_ref, o_ref,
                 ppad, rt_s, at_s, bt_s, kt_s, bh_s, kh_s, v_s, pt_s, y_s, state):
    TB, L = TB_RWKV, RWKV_CHUNK
    t_idx = pl.program_id(1)

    @pl.when(t_idx == 0)
    def _():
        ppad[0:SUBLANES, :] = jnp.zeros((SUBLANES, D_R), F32)
        state[...] = jnp.zeros(state.shape, F32)

    ppad[SUBLANES:SUBLANES + TB, :] = p_ref[...]
    p = p_ref[...]
    prev = ppad[SUBLANES - 1:SUBLANES - 1 + TB, :]
    xs = p + (prev - p) * mu_ref[...]
    ppad[0:SUBLANES, :] = ppad[TB:TB + SUBLANES, :]

    r = xs[:, 0:D_RWKV]
    k = xs[:, D_RWKV:2 * D_RWKV]
    v = xs[:, 2 * D_RWKV:3 * D_RWKV]
    xw = xs[:, 3 * D_RWKV:3 * D_RWKV + LANES]
    xa = xs[:, 3 * D_RWKV + LANES:3 * D_RWKV + 2 * LANES]
    xg = xs[:, 3 * D_RWKV + 2 * LANES:D_R]

    w = _log_sigmoid(w0_ref[...] + _dot(jnp.tanh(xw), ww2_ref[...])) - 0.5
    a = jax.nn.sigmoid(a0_ref[...] + _dot(xa, wa2_ref[...]))
    g = _dot(jax.nn.sigmoid(xg), wg2_ref[...])
    seg = seg_ref[...]
    kk = k * kk_ref[...]
    kk = kk / jnp.maximum(jnp.sqrt(_dot_sel_r(kk * kk, seg)), 1e-12)
    k2 = k * (1.0 + (a - 1.0) * ka_ref[...])
    bonus = _dot_sel_r(r * k2 * rk_ref[...], seg)

    lw = -jnp.exp(w)
    cum = _dot_sel_l(tri_ref[...], lw)
    tot = _dot_sel_l(ones_ref[...], lw)
    e_neg = jnp.exp(-cum)
    e_end = jnp.exp(tot - cum)
    b = kk * a
    rt_s[...] = r * jnp.exp(cum)
    at_s[...] = -kk * jnp.exp(cum - lw)
    bt_s[...] = b * e_neg
    kt_s[...] = k2 * e_neg
    bh_s[...] = b * e_end
    kh_s[...] = k2 * e_end
    v_s[...] = v
    pt_s[...] = jnp.exp(tot)

    lane = lax.broadcasted_iota(jnp.int32, (1, LANES), 1)
    m0 = (lane < RWKV_HEAD).astype(F32)
    m1 = 1.0 - m0
    ri = lax.broadcasted_iota(jnp.int32, (2 * L, 2 * L), 0)
    ci = lax.broadcasted_iota(jnp.int32, (2 * L, 2 * L), 1)
    same = (ri // L) == (ci // L)
    tril_s = (same & (ri > ci)).astype(F32)
    tril_i = (same & (ri >= ci)).astype(F32)
    eye = (ri == ci).astype(F32)
    blockdiag = ((ri // RWKV_HEAD) == (ci // RWKV_HEAD)).astype(F32)
    base_mask = ((ri // INV_BASE) == (ci // INV_BASE)).astype(F32)
    level_masks = []
    size = INV_BASE
    while size < L:
        level_masks.append((((ri // (2 * size)) == (ci // (2 * size)))
                            & ((ri // size) != (ci // size))).astype(F32))
        size *= 2

    def stack(t):
        return jnp.concatenate([t * m0, t * m1], axis=0)

    def chunk(c, carry):
        rows = pl.ds(pl.multiple_of(c * L, L), L)
        for pr in range(RWKV_PAIRS):
            cols = slice(pr * LANES, (pr + 1) * LANES)
            a_st = stack(at_s[rows, cols])
            r_st = stack(rt_s[rows, cols])
            b_st = stack(bt_s[rows, cols])
            k_st = stack(kt_s[rows, cols])
            v_st = stack(v_s[rows, cols])
            bh_st = stack(bh_s[rows, cols])
            kh_st = stack(kh_s[rows, cols])
            n_ab = _dot_nt(a_st, b_st) * tril_s
            a_ak = _dot_nt(a_st, k_st) * tril_s
            a_rb = _dot_nt(r_st, b_st) * tril_i
            a_rk = _dot_nt(r_st, k_st) * tril_i
            n0 = n_ab * base_mask
            inv = eye + n0
            pw = n0
            for _ in range(INV_BASE_SQUARINGS):
                pw = _dot3(pw, pw)
                inv = inv + _dot3(inv, pw)
            for lvl_mask in level_masks:
                inv = inv + _dot(inv, _dot(n_ab * lvl_mask, inv))
            a_p = _dot(inv, a_st)
            u_pre = _dot(inv, _dot(a_ak, v_st))
            r_p = r_st + _dot(a_rb, a_p)
            y_pre = _dot(a_rb, u_pre) + _dot(a_rk, v_st)
            ptot = pt_s[pl.ds(pl.multiple_of(c * L, L), 1), cols]
            gmat = eye * ptot + _dot_tn(bh_st, a_p)
            t_pre = _dot_tn(bh_st, u_pre) + _dot_tn(kh_st, v_st)
            st = state[pr]
            y_st = _dot(r_p, st) + y_pre
            state[pr] = (_dot(gmat, st) + t_pre) * blockdiag
            y_s[rows, cols] = y_st[0:L, :] + y_st[L:2 * L, :]
        return carry

    lax.fori_loop(0, TB // L, chunk, 0)

    y = y_s[...]
    mean = _dot_sel_r(y, seg) * (1.0 / RWKV_HEAD)
    yc = y - mean
    var = _dot_sel_r(yc * yc, seg) * (1.0 / RWKV_HEAD)
    yn = yc * lax.rsqrt(var + LNX_EPS) * lnw_ref[...] + lnb_ref[...]
    o_ref[...] = ((yn + bonus * v) * g).astype(o_ref.dtype)


def _rwkv(p_r, mu, ww2, w0, wa2, a0, wg2, k_k, k_a, r_k, lnx_w, lnx_b):
    B, T, _ = p_r.shape
    TB, L = TB_RWKV, RWKV_CHUNK
    i512 = jnp.arange(D_RWKV)
    seg = (i512[:, None] // RWKV_HEAD == i512[None, :] // RWKV_HEAD).astype(BF16)
    it = jnp.arange(TB)
    same_chunk = it[:, None] // L == it[None, :] // L
    tri = (same_chunk & (it[:, None] >= it[None, :])).astype(BF16)
    ones = same_chunk.astype(BF16)
    row = lambda n: _const_spec((1, n))
    blk = lambda n: pl.BlockSpec((None, TB, n), lambda b, t: (b, t, 0))
    scr = lambda: pltpu.VMEM((TB, D_RWKV), F32)
    return pl.pallas_call(
        _rwkv_kernel,
        grid=(B, T // TB),
        in_specs=[blk(D_R), row(D_R), _const_spec((LANES, D_RWKV)), row(D_RWKV),
                  _const_spec((LANES, D_RWKV)), row(D_RWKV), _const_spec((G_LORA, D_RWKV)),
                  row(D_RWKV), row(D_RWKV), row(D_RWKV), row(D_RWKV), row(D_RWKV),
                  _const_spec((D_RWKV, D_RWKV)), _const_spec((TB, TB)), _const_spec((TB, TB))],
        out_specs=blk(D_RWKV),
        out_shape=jax.ShapeDtypeStruct((B, T, D_RWKV), BF16),
        scratch_shapes=[pltpu.VMEM((TB + SUBLANES, D_R), F32)] + [scr() for _ in range(9)]
                       + [pltpu.VMEM((RWKV_PAIRS, LANES, LANES), F32)],
        compiler_params=pltpu.CompilerParams(dimension_semantics=("arbitrary", "arbitrary"),
                                             vmem_limit_bytes=VMEM_LIMIT),
        name="rwkv",
    )(p_r, mu, ww2, w0, wa2, a0, wg2, k_k, k_a, r_k, lnx_w, lnx_b, seg, tri, ones)


def _mlstm_kernel(pm_ref, pg_ref, grow_ref, cw_ref, cb_ref, gbc_ref, gbr_ref, mhn_ref, tri_ref,
                  trit_ref, o_ref, xpad, q_s, k_s, h_s, c_state, n_state, m_state):
    TB, L, H, N = TB_MLSTM, MLSTM_CHUNK, MLSTM_HEADS, MLSTM_HEAD
    DQK = 2 * D_MLSTM
    t_idx = pl.program_id(1)

    @pl.when(t_idx == 0)
    def _():
        xpad[0:SUBLANES, :] = jnp.zeros((SUBLANES, DQK), F32)
        c_state[...] = jnp.zeros(c_state.shape, F32)
        n_state[...] = jnp.zeros(n_state.shape, F32)
        m_state[...] = jnp.zeros(m_state.shape, F32)

    xpad[SUBLANES:SUBLANES + TB, :] = pm_ref[:, 0:DQK]
    acc = cb_ref[...] + cw_ref[QK_CONV - 1:QK_CONV, :] * pm_ref[:, 0:DQK]
    for j in range(QK_CONV - 1):
        d = QK_CONV - 1 - j
        acc = acc + cw_ref[j:j + 1, :] * xpad[SUBLANES - d:SUBLANES - d + TB, :]
    xpad[0:SUBLANES, :] = xpad[TB:TB + SUBLANES, :]
    qk = _silu(acc)
    q_s[...] = qk[:, 0:D_MLSTM]
    k_s[...] = qk[:, D_MLSTM:DQK] * (N ** -0.5)

    lane = lax.broadcasted_iota(jnp.int32, (1, LANES), 1)
    gc = pg_ref[...] + gbc_ref[...]
    gc = jnp.where(lane < H, gc, jnp.where(lane < 2 * H, _log_sigmoid(gc), 0.0))
    cum_c = _dot_sel_l(tri_ref[...], gc)
    gr = grow_ref[...] + gbr_ref[:, 0:1]
    sub = lax.broadcasted_iota(jnp.int32, (N_GATES, 1), 0)
    gr = jnp.where(sub < H, gr, _log_sigmoid(gr))
    cum_r = _dot_sel_r(gr, trit_ref[...])

    ti = lax.broadcasted_iota(jnp.int32, (L, L), 0)
    si = lax.broadcasted_iota(jnp.int32, (L, L), 1)
    causal = ti >= si

    for c in range(TB // L):
        rows = slice(c * L, (c + 1) * L)
        for h in range(H):
            cols = slice(h * N, (h + 1) * N)
            q = q_s[rows, cols]
            k = k_s[rows, cols]
            v = pm_ref[rows, 2 * D_MLSTM + h * N:2 * D_MLSTM + (h + 1) * N]
            b_col = cum_c[rows, H + h:H + h + 1]
            ig_col = gc[rows, h:h + 1]
            b_row = cum_r[H + h:H + h + 1, rows]
            ig_row = gr[h:h + 1, rows]
            m_prev = m_state[h][:, 0:1]
            dmat = jnp.where(causal, b_col - b_row + ig_row, -jnp.inf)
            inter = b_col + m_prev
            m_t = jnp.maximum(inter, jnp.max(dmat, axis=-1, keepdims=True))
            dw = jnp.exp(dmat - m_t)
            iw = jnp.exp(inter - m_t)
            s = _dot_nt(q, k) * dw
            cst = c_state[h]
            nst = n_state[h]
            num = _dot(s, v) + iw * _dot(q, cst)
            den = jnp.sum(s, axis=-1, keepdims=True) + iw * jnp.sum(q * nst, axis=-1, keepdims=True)
            h_s[rows, cols] = num / jnp.maximum(jnp.abs(den), jnp.exp(-m_t))
            b_end = b_col[L - 1:L, :]
            gl_col = b_end - b_col + ig_col
            m_new = jnp.maximum(b_end + m_prev, jnp.max(gl_col, axis=0, keepdims=True))
            ws = jnp.exp(gl_col - m_new)
            dec = jnp.exp(b_end + m_prev - m_new)
            c_state[h] = dec * cst + _dot_tn(k, ws * v)
            n_state[h] = dec * nst + jnp.sum(ws * k, axis=0, keepdims=True)
            m_state[h] = jnp.broadcast_to(m_new, (1, LANES))

    hh = h_s[...]
    o = pm_ref[:, 3 * D_MLSTM:4 * D_MLSTM]
    outs = []
    for h in range(H):
        y = hh[:, h * N:(h + 1) * N]
        mu = jnp.mean(y, axis=-1, keepdims=True)
        yc = y - mu
        var = jnp.mean(yc * yc, axis=-1, keepdims=True)
        outs.append(yc * lax.rsqrt(var + MHN_EPS))
    yn = jnp.concatenate(outs, axis=-1) * mhn_ref[...]
    o_ref[...] = (yn * jax.nn.sigmoid(o)).astype(o_ref.dtype)


def _mlstm(p_m, p_g, g_row, conv_w, conv_b, gb_col, gb_row, mhn_w):
    B, T, _ = p_m.shape
    TB, L = TB_MLSTM, MLSTM_CHUNK
    it = jnp.arange(TB)
    same_chunk = it[:, None] // L == it[None, :] // L
    tri = (same_chunk & (it[:, None] >= it[None, :])).astype(BF16)
    blk = lambda n: pl.BlockSpec((None, TB, n), lambda b, t: (b, t, 0))
    return pl.pallas_call(
        _mlstm_kernel,
        grid=(B, T // TB),
        in_specs=[blk(D_M), blk(LANES),
                  pl.BlockSpec((None, N_GATES, TB), lambda b, t: (b, 0, t)),
                  _const_spec((QK_CONV, 2 * D_MLSTM)), _const_spec((1, 2 * D_MLSTM)),
                  _const_spec((1, LANES)), _const_spec((N_GATES, LANES)),
                  _const_spec((1, D_MLSTM)), _const_spec((TB, TB)), _const_spec((TB, TB))],
        out_specs=blk(D_MLSTM),
        out_shape=jax.ShapeDtypeStruct((B, T, D_MLSTM), BF16),
        scratch_shapes=[pltpu.VMEM((TB + SUBLANES, 2 * D_MLSTM), F32),
                        pltpu.VMEM((TB, D_MLSTM), F32), pltpu.VMEM((TB, D_MLSTM), F32),
                        pltpu.VMEM((TB, D_MLSTM), F32),
                        pltpu.VMEM((MLSTM_HEADS, MLSTM_HEAD, MLSTM_HEAD), F32),
                        pltpu.VMEM((MLSTM_HEADS, 1, MLSTM_HEAD), F32),
                        pltpu.VMEM((MLSTM_HEADS, 1, LANES), F32)],
        compiler_params=pltpu.CompilerParams(dimension_semantics=("arbitrary", "arbitrary"),
                                             vmem_limit_bytes=VMEM_LIMIT),
        name="mlstm",
    )(p_m, p_g, g_row, conv_w, conv_b, gb_col, gb_row, mhn_w, tri, tri.T)


def _ffn_kernel(x_ref, yr_ref, ym_ref, mod_ref, wout_ref, n2g_ref, wup_ref, cw_ref, cb_ref,
                wdn_ref, nfg_ref, o_ref, upad_a, upad_l, tail_a, tail_l, acc_s):
    TM = TM_FFN
    t_idx = pl.program_id(1)

    @pl.when(t_idx == 0)
    def _():
        tail_a[...] = jnp.zeros(tail_a.shape, F32)
        tail_l[...] = jnp.zeros(tail_l.shape, F32)

    y = (jnp.dot(yr_ref[...], wout_ref[0:D_RWKV, :], preferred_element_type=F32)
         + jnp.dot(ym_ref[...], wout_ref[D_RWKV:2 * D_RWKV, :], preferred_element_type=F32))
    x1 = x_ref[...] + mod_ref[2:3, :] * y
    h = _rms_mod(x1, n2g_ref[...], mod_ref[3:4, :], mod_ref[4:5, :]).astype(BF16)

    def conv(u, upad, tail, col0):
        cols = slice(col0, col0 + FF_TILE)
        upad[0:SUBLANES, :] = tail[:, cols]
        upad[SUBLANES:SUBLANES + TM, :] = u
        tail[:, cols] = u[TM - SUBLANES:TM, :]
        out = cb_ref[:, cols] + cw_ref[FF_CONV - 1:FF_CONV, cols] * u
        for j in range(FF_CONV - 1):
            d = FF_CONV - 1 - j
            out = out + cw_ref[j:j + 1, cols] * upad[SUBLANES - d:SUBLANES - d + TM, :]
        return out

    acc_s[...] = jnp.zeros(acc_s.shape, F32)
    for j in range(D_FF // FF_TILE):
        ca = j * FF_TILE
        cl = D_FF + j * FF_TILE
        ua = jnp.dot(h, wup_ref[:, ca:ca + FF_TILE], preferred_element_type=F32)
        ul = jnp.dot(h, wup_ref[:, cl:cl + FF_TILE], preferred_element_type=F32)
        act = _silu(conv(ua, upad_a, tail_a, ca)) * conv(ul, upad_l, tail_l, cl)
        acc_s[...] += jnp.dot(act.astype(BF16), wdn_ref[ca:ca + FF_TILE, :],
                              preferred_element_type=F32)

    x2 = x1 + mod_ref[5:6, :] * acc_s[...]
    o_ref[...] = (x2 * lax.rsqrt(jnp.mean(x2 * x2, axis=-1, keepdims=True) + NORM_EPS)
                  * nfg_ref[...])


def _ffn(x, y_r, y_m, mod, w_out, norm2_g, w_up, conv_w, conv_b, w_down, normf_g):
    B, T, _ = x.shape
    TM = TM_FFN
    blk = lambda n: pl.BlockSpec((None, TM, n), lambda b, t: (b, t, 0))
    return pl.pallas_call(
        _ffn_kernel,
        grid=(B, T // TM),
        in_specs=[blk(D_MODEL), blk(D_RWKV), blk(D_MLSTM),
                  pl.BlockSpec((None, 6, D_MODEL), lambda b, t: (b, 0, 0)),
                  _const_spec((D_MODEL, D_MODEL)), _const_spec((1, D_MODEL)),
                  _const_spec((D_MODEL, 2 * D_FF)), _const_spec((FF_CONV, 2 * D_FF)),
                  _const_spec((1, 2 * D_FF)), _const_spec((D_FF, D_MODEL)),
                  _const_spec((1, D_MODEL))],
        out_specs=blk(D_MODEL),
        out_shape=jax.ShapeDtypeStruct((B, T, D_MODEL), F32),
        scratch_shapes=[pltpu.VMEM((TM + SUBLANES, FF_TILE), F32),
                        pltpu.VMEM((TM + SUBLANES, FF_TILE), F32),
                        pltpu.VMEM((SUBLANES, 2 * D_FF), F32),
                        pltpu.VMEM((SUBLANES, 2 * D_FF), F32),
                        pltpu.VMEM((TM, D_MODEL), F32)],
        compiler_params=pltpu.CompilerParams(dimension_semantics=("arbitrary", "arbitrary"),
                                             vmem_limit_bytes=VMEM_LIMIT),
        name="ffn",
    )(x, y_r, y_m, mod, w_out, norm2_g, w_up, conv_w, conv_b, w_down, normf_g)


def _pad_cols(a, n):
    return jnp.pad(a, ((0, 0), (0, n - a.shape[1])))


def _pad_rows(a, n):
    return jnp.pad(a, ((0, n - a.shape[0]), (0, 0)))


def kernel(x, c, w_ada, b_ada, norm1_g, norm2_g, normf_g, w_in, mu_rwkv, w_w2, w0, w_a2, a0, w_g2,
           k_k, k_a, r_k, lnx_w, lnx_b, conv_qk_w, conv_qk_b, i_bias, f_bias, mhn_w, w_out, w_up,
           conv_ff_w, conv_ff_b, w_down):
    assert w_ada.shape[0] == 1, "single-layer block"
    B = x.shape[0]
    row = lambda a: a.reshape(1, -1)
    R3 = 3 * D_RWKV
    d_rin = R3 + W_LORA + A_LORA + G_LORA

    wi = w_in[0]

    def rwkv_cols(a):
        return jnp.concatenate([a[:, 0:R3], _pad_cols(a[:, R3:R3 + W_LORA], LANES),
                                _pad_cols(a[:, R3 + W_LORA:R3 + W_LORA + A_LORA], LANES),
                                a[:, R3 + W_LORA + A_LORA:d_rin]], axis=1)

    w_gate = wi[:, d_rin + D_M:]
    w_all = jnp.concatenate([rwkv_cols(wi[:, :d_rin]), wi[:, d_rin:d_rin + D_M],
                             _pad_cols(w_gate, LANES)], axis=1).astype(BF16)
    w_gt = w_gate.T.astype(BF16)
    mu = rwkv_cols(row(mu_rwkv[0]))
    gates_b = jnp.concatenate([i_bias[0], f_bias[0]])
    gb_col = _pad_cols(row(gates_b), LANES)
    gb_row = jnp.broadcast_to(gates_b[:, None], (N_GATES, LANES))

    mod = _ada(c, w_ada[0], b_ada[0])
    mod = jnp.transpose(mod, (1, 0, 2))
    p_r, p_m, p_g, g_row = _inproj(x, mod, row(norm1_g[0]), w_all, w_gt)
    y_r = _rwkv(p_r, mu, _pad_rows(w_w2[0], LANES).astype(BF16), row(w0[0]),
                _pad_rows(w_a2[0], LANES).astype(BF16), row(a0[0]), w_g2[0].astype(BF16),
                row(k_k[0]), row(k_a[0]), row(r_k[0]), row(lnx_w[0]), row(lnx_b[0]))
    y_m = _mlstm(p_m, p_g, g_row, conv_qk_w[0], row(conv_qk_b[0]), gb_col, gb_row, row(mhn_w[0]))
    return _ffn(x, y_r, y_m, mod, w_out[0].astype(BF16), row(norm2_g[0]), w_up[0].astype(BF16),
                conv_ff_w[0], row(conv_ff_b[0]), w_down[0].astype(BF16), row(normf_g))
```

```python
import functools

import jax
import jax.numpy as jnp
from jax import lax
from jax.experimental import pallas as pl
from jax.experimental.pallas import tpu as pltpu

D_MODEL = 1024
D_RWKV = 512
RWKV_HEAD = 64
RWKV_HEADS = 8
RWKV_PAIRS = RWKV_HEADS // 2
W_LORA = 64
A_LORA = 64
G_LORA = 128
D_MLSTM = 512
MLSTM_HEADS = 4
MLSTM_HEAD = 128
QK_CONV = 4
MLSTM_CHUNK = 64
D_FF = 2816
FF_CONV = 3
NORM_EPS = 1e-6
LNX_EPS = 64e-5
MHN_EPS = 1e-6

LANES = 128
SUBLANES = 8
RWKV_CHUNK = 64
INV_BASE = 8
INV_BASE_SQUARINGS = 2
RWKV_GROUP = 8
D_R = 3 * D_RWKV + 3 * LANES
D_M = 4 * D_MLSTM
D_P = D_R + D_M + LANES
N_GATES = 2 * MLSTM_HEADS

TM_IN = 512
TB_RWKV = 256
TB_MLSTM = 256
TM_FFN = 512
FF_TILE = 256
VMEM_LIMIT = 56 * 1024 * 1024

F32 = jnp.float32
BF16 = jnp.bfloat16


def _dot(a, b):
    return jnp.dot(a.astype(BF16), b.astype(BF16), preferred_element_type=F32)


def _dot_nt(a, b):
    return lax.dot_general(a.astype(BF16), b.astype(BF16), (((1,), (1,)), ((), ())),
                           preferred_element_type=F32)


def _dot_tn(a, b):
    return lax.dot_general(a.astype(BF16), b.astype(BF16), (((0,), (0,)), ((), ())),
                           preferred_element_type=F32)


def _dot3(a, b):
    a_hi = a.astype(BF16)
    a_lo = (a - a_hi.astype(F32)).astype(BF16)
    b_hi = b.astype(BF16)
    b_lo = (b - b_hi.astype(F32)).astype(BF16)
    d = lambda u, w: jnp.dot(u, w, preferred_element_type=F32)
    return d(a_hi, b_hi) + (d(a_hi, b_lo) + d(a_lo, b_hi))


def _split3(x):
    hi = x.astype(BF16)
    r1 = x - hi.astype(F32)
    mid = r1.astype(BF16)
    lo = (r1 - mid.astype(F32)).astype(BF16)
    return hi, mid, lo


def _dot_sel_l(sel, x):
    hi, mid, lo = _split3(x)
    d = lambda t: jnp.dot(sel, t, preferred_element_type=F32)
    return d(hi) + (d(mid) + d(lo))


def _dot_sel_r(x, sel):
    hi, mid, lo = _split3(x)
    d = lambda t: jnp.dot(t, sel, preferred_element_type=F32)
    return d(hi) + (d(mid) + d(lo))


def _log_sigmoid(z):
    return jnp.minimum(z, 0.0) - jnp.log1p(jnp.exp(-jnp.abs(z)))


def _silu(z):
    return z * jax.nn.sigmoid(z)


def _const_spec(shape):
    nd = len(shape)
    return pl.BlockSpec(shape, lambda *_: (0,) * nd, pipeline_mode=pl.Buffered(1))


def _ada_kernel(c_ref, w_ref, b_ref, o_ref):
    s = _silu(c_ref[...])
    o_ref[0] = jnp.dot(s, w_ref[...], preferred_element_type=F32,
                       precision=lax.Precision.HIGHEST) + b_ref[0]


def _ada(c, w_ada, b_ada):
    B = c.shape[0]
    return pl.pallas_call(
        _ada_kernel,
        grid=(6,),
        in_specs=[pl.BlockSpec((B, D_MODEL), lambda j: (0, 0)),
                  pl.BlockSpec((D_MODEL, D_MODEL), lambda j: (0, j)),
                  pl.BlockSpec((1, 1, D_MODEL), lambda j: (j, 0, 0))],
        out_specs=pl.BlockSpec((1, B, D_MODEL), lambda j: (j, 0, 0)),
        out_shape=jax.ShapeDtypeStruct((6, B, D_MODEL), F32),
        compiler_params=pltpu.CompilerParams(dimension_semantics=("arbitrary",),
                                             vmem_limit_bytes=VMEM_LIMIT),
        name="ada",
    )(c, w_ada, b_ada.reshape(6, 1, D_MODEL))


def _rms_mod(x, g, shift, scale):
    y = x * lax.rsqrt(jnp.mean(x * x, axis=-1, keepdims=True) + NORM_EPS) * g
    return y * (1.0 + scale) + shift


def _inproj_kernel(x_ref, mod_ref, g_ref, w_ref, wgt_ref, pr_ref, pm_ref, pg_ref, grow_ref):
    h = _rms_mod(x_ref[...], g_ref[...], mod_ref[0:1, :], mod_ref[1:2, :]).astype(BF16)
    pr_ref[...] = jnp.dot(h, w_ref[:, 0:D_R], preferred_element_type=F32)
    pm_ref[...] = jnp.dot(h, w_ref[:, D_R:D_R + D_M], preferred_element_type=F32)
    pg_ref[...] = jnp.dot(h, w_ref[:, D_R + D_M:D_P], preferred_element_type=F32)
    grow_ref[...] = lax.dot_general(wgt_ref[...], h, (((1,), (1,)), ((), ())),
                                    preferred_element_type=F32)


def _inproj(x, mod, norm_g, w_all, w_gt):
    B, T, _ = x.shape
    return pl.pallas_call(
        _inproj_kernel,
        grid=(B, T // TM_IN),
        in_specs=[pl.BlockSpec((None, TM_IN, D_MODEL), lambda b, t: (b, t, 0)),
                  pl.BlockSpec((None, 6, D_MODEL), lambda b, t: (b, 0, 0)),
                  _const_spec((1, D_MODEL)),
                  _const_spec((D_MODEL, D_P)),
                  _const_spec((N_GATES, D_MODEL))],
        out_specs=[pl.BlockSpec((None, TM_IN, D_R), lambda b, t: (b, t, 0)),
                   pl.BlockSpec((None, TM_IN, D_M), lambda b, t: (b, t, 0)),
                   pl.BlockSpec((None, TM_IN, LANES), lambda b, t: (b, t, 0)),
                   pl.BlockSpec((None, N_GATES, TM_IN), lambda b, t: (b, 0, t))],
        out_shape=[jax.ShapeDtypeStruct((B, T, D_R), F32),
                   jax.ShapeDtypeStruct((B, T, D_M), F32),
                   jax.ShapeDtypeStruct((B, T, LANES), F32),
                   jax.ShapeDtypeStruct((B, N_GATES, T), F32)],
        compiler_params=pltpu.CompilerParams(dimension_semantics=("arbitrary", "arbitrary"),
                                             vmem_limit_bytes=VMEM_LIMIT),
        name="inproj",
    )(x, mod, norm_g, w_all, w_gt)


def _rwkv_kernel(p_ref, mu_ref, ww2_ref, w0_ref, wa2_ref, a0_ref, wg2_ref, kk_ref, ka_ref, rk_ref,
                 lnw_ref, lnb_ref, seg_ref, tri_ref, ones_ref, o_ref,
                 ppad, rt_s, at_s, bt_s, kt_s, bh_s, kh_s, v_s, pt_s, y_s, state,
                 rp_s, yp_s, gm_s, tp_s):
    TB, L = TB_RWKV, RWKV_CHUNK
    t_idx = pl.program_id(1)

    @pl.when(t_idx == 0)
    def _():
        ppad[0:SUBLANES, :] = jnp.zeros((SUBLANES, D_R), F32)
        state[...] = jnp.zeros(state.shape, F32)

    ppad[SUBLANES:SUBLANES + TB, :] = p_ref[...]
    p = p_ref[...]
    prev = ppad[SUBLANES - 1:SUBLANES - 1 + TB, :]
    xs = p + (prev - p) * mu_ref[...]
    ppad[0:SUBLANES, :] = ppad[TB:TB + SUBLANES, :]

    r = xs[:, 0:D_RWKV]
    k = xs[:, D_RWKV:2 * D_RWKV]
    v = xs[:, 2 * D_RWKV:3 * D_RWKV]
    xw = xs[:, 3 * D_RWKV:3 * D_RWKV + LANES]
    xa = xs[:, 3 * D_RWKV + LANES:3 * D_RWKV + 2 * LANES]
    xg = xs[:, 3 * D_RWKV + 2 * LANES:D_R]

    w = _log_sigmoid(w0_ref[...] + _dot(jnp.tanh(xw), ww2_ref[...])) - 0.5
    a = jax.nn.sigmoid(a0_ref[...] + _dot(xa, wa2_ref[...]))
    g = _dot(jax.nn.sigmoid(xg), wg2_ref[...])
    seg = seg_ref[...]
    kk = k * kk_ref[...]
    kk = kk / jnp.maximum(jnp.sqrt(_dot_sel_r(kk * kk, seg)), 1e-12)
    k2 = k * (1.0 + (a - 1.0) * ka_ref[...])
    bonus = _dot_sel_r(r * k2 * rk_ref[...], seg)

    lw = -jnp.exp(w)
    cum = _dot_sel_l(tri_ref[...], lw)
    tot = _dot_sel_l(ones_ref[...], lw)
    e_neg = jnp.exp(-cum)
    e_end = jnp.exp(tot - cum)
    b = kk * a
    rt_s[...] = r * jnp.exp(cum)
    at_s[...] = -kk * jnp.exp(cum - lw)
    bt_s[...] = b * e_neg
    kt_s[...] = k2 * e_neg
    bh_s[...] = b * e_end
    kh_s[...] = k2 * e_end
    v_s[...] = v
    pt_s[...] = jnp.exp(tot)

    lane = lax.broadcasted_iota(jnp.int32, (1, LANES), 1)
    m0 = (lane < RWKV_HEAD).astype(F32)
    m1 = 1.0 - m0
    ri = lax.broadcasted_iota(jnp.int32, (2 * L, 2 * L), 0)
    ci = lax.broadcasted_iota(jnp.int32, (2 * L, 2 * L), 1)
    same = (ri // L) == (ci // L)
    tril_s = (same & (ri > ci)).astype(F32)
    tril_i = (same & (ri >= ci)).astype(F32)
    eye = (ri == ci).astype(F32)
    blockdiag = ((ri // RWKV_HEAD) == (ci // RWKV_HEAD)).astype(F32)
    base_mask = ((ri // INV_BASE) == (ci // INV_BASE)).astype(F32)
    level_masks = []
    size = INV_BASE
    while size < L:
        level_masks.append((((ri // (2 * size)) == (ci // (2 * size)))
                            & ((ri // size) != (ci // size))).astype(F32))
        size *= 2

    def stack(t):
        return jnp.concatenate([t * m0, t * m1], axis=0)

    def chunk_tables(unit):
        c, pr = unit
        rows = slice(c * L, (c + 1) * L)
        cols = slice(pr * LANES, (pr + 1) * LANES)
        return dict(a=stack(at_s[rows, cols]), r=stack(rt_s[rows, cols]), b=stack(bt_s[rows, cols]),
                    k=stack(kt_s[rows, cols]), v=stack(v_s[rows, cols]), bh=stack(bh_s[rows, cols]),
                    kh=stack(kh_s[rows, cols]), ptot=pt_s[c * L:c * L + 1, cols])

    units = [(c, pr) for c in range(TB // L) for pr in range(RWKV_PAIRS)]
    for g0 in range(0, len(units), RWKV_GROUP):
        group = units[g0:g0 + RWKV_GROUP]
        tabs = [chunk_tables(u) for u in group]
        prods = [_dot_nt(jnp.concatenate([t["a"], t["r"]], axis=0),
                         jnp.concatenate([t["b"], t["k"]], axis=0)) for t in tabs]
        n_ab = [p[0:2 * L, 0:2 * L] * tril_s for p in prods]
        a_ak = [p[0:2 * L, 2 * L:4 * L] * tril_s for p in prods]
        a_rb = [p[2 * L:4 * L, 0:2 * L] * tril_i for p in prods]
        a_rk = [p[2 * L:4 * L, 2 * L:4 * L] * tril_i for p in prods]
        pw = [n * base_mask for n in n_ab]
        inv = [eye + n for n in pw]
        for _ in range(INV_BASE_SQUARINGS):
            pw = [_dot3(x, x) for x in pw]
            inv = [m + _dot3(m, x) for m, x in zip(inv, pw)]
        for lvl_mask in level_masks:
            cm = [_dot(n * lvl_mask, m) for n, m in zip(n_ab, inv)]
            inv = [m + _dot(m, x) for m, x in zip(inv, cm)]
        av = [_dot(jnp.concatenate([x, y], axis=0), t["v"]) for x, y, t in zip(a_ak, a_rk, tabs)]
        w = [_dot(m, jnp.concatenate([t["a"], x[0:2 * L]], axis=1)) for m, t, x in zip(inv, tabs, av)]
        q = [_dot(x, y) for x, y in zip(a_rb, w)]
        tn = [_dot_tn(t["bh"], y) for t, y in zip(tabs, w)]
        kv = [_dot_tn(t["kh"], t["v"]) for t in tabs]
        for i, (c, pr) in enumerate(group):
            u = c * RWKV_PAIRS + pr
            rp_s[u] = tabs[i]["r"] + q[i][:, 0:LANES]
            yp_s[u] = q[i][:, LANES:2 * LANES] + av[i][2 * L:4 * L]
            gm_s[u] = eye * tabs[i]["ptot"] + tn[i][:, 0:LANES]
            tp_s[u] = tn[i][:, LANES:2 * LANES] + kv[i]

    for c in range(TB // L):
        us = [c * RWKV_PAIRS + pr for pr in range(RWKV_PAIRS)]
        sts = [state[pr] for pr in range(RWKV_PAIRS)]
        both = [_dot(jnp.concatenate([rp_s[u], gm_s[u]], axis=0), st) for u, st in zip(us, sts)]
        for pr, (u, z) in enumerate(zip(us, both)):
            y_st = z[0:2 * L] + yp_s[u]
            state[pr] = (z[2 * L:4 * L] + tp_s[u]) * blockdiag
            y_s[c * L:(c + 1) * L, pr * LANES:(pr + 1) * LANES] = y_st[0:L, :] + y_st[L:2 * L, :]

    y = y_s[...]
    mean = _dot_sel_r(y, seg) * (1.0 / RWKV_HEAD)
    yc = y - mean
    var = _dot_sel_r(yc * yc, seg) * (1.0 / RWKV_HEAD)
    yn = yc * lax.rsqrt(var + LNX_EPS) * lnw_ref[...] + lnb_ref[...]
    o_ref[...] = ((yn + bonus * v) * g).astype(o_ref.dtype)


def _rwkv(p_r, mu, ww2, w0, wa2, a0, wg2, k_k, k_a, r_k, lnx_w, lnx_b):
    B, T, _ = p_r.shape
    TB, L = TB_RWKV, RWKV_CHUNK
    i512 = jnp.arange(D_RWKV)
    seg = (i512[:, None] // RWKV_HEAD == i512[None, :] // RWKV_HEAD).astype(BF16)
    it = jnp.arange(TB)
    same_chunk = it[:, None] // L == it[None, :] // L
    tri = (same_chunk & (it[:, None] >= it[None, :])).astype(BF16)
    ones = same_chunk.astype(BF16)
    row = lambda n: _const_spec((1, n))
    blk = lambda n: pl.BlockSpec((None, TB, n), lambda b, t: (b, t, 0))
    scr = lambda: pltpu.VMEM((TB, D_RWKV), F32)
    return pl.pallas_call(
        _rwkv_kernel,
        grid=(B, T // TB),
        in_specs=[blk(D_R), row(D_R), _const_spec((LANES, D_RWKV)), row(D_RWKV),
                  _const_spec((LANES, D_RWKV)), row(D_RWKV), _const_spec((G_LORA, D_RWKV)),
                  row(D_RWKV), row(D_RWKV), row(D_RWKV), row(D_RWKV), row(D_RWKV),
                  _const_spec((D_RWKV, D_RWKV)), _const_spec((TB, TB)), _const_spec((TB, TB))],
        out_specs=blk(D_RWKV),
        out_shape=jax.ShapeDtypeStruct((B, T, D_RWKV), BF16),
        scratch_shapes=[pltpu.VMEM((TB + SUBLANES, D_R), F32)] + [scr() for _ in range(9)]
                       + [pltpu.VMEM((RWKV_PAIRS, LANES, LANES), F32)]
                       + [pltpu.VMEM(((TB // L) * RWKV_PAIRS, 2 * L, LANES), F32) for _ in range(4)],
        compiler_params=pltpu.CompilerParams(dimension_semantics=("arbitrary", "arbitrary"),
                                             vmem_limit_bytes=VMEM_LIMIT),
        name="rwkv",
    )(p_r, mu, ww2, w0, wa2, a0, wg2, k_k, k_a, r_k, lnx_w, lnx_b, seg, tri, ones)


def _mlstm_kernel(pm_ref, pg_ref, grow_ref, cw_ref, cb_ref, gbc_ref, gbr_ref, mhn_ref, tri_ref,
                  trit_ref, o_ref, xpad, q_s, k_s, h_s, c_state, n_state, m_state):
    TB, L, H, N = TB_MLSTM, MLSTM_CHUNK, MLSTM_HEADS, MLSTM_HEAD
    DQK = 2 * D_MLSTM
    t_idx = pl.program_id(1)

    @pl.when(t_idx == 0)
    def _():
        xpad[0:SUBLANES, :] = jnp.zeros((SUBLANES, DQK), F32)
        c_state[...] = jnp.zeros(c_state.shape, F32)
        n_state[...] = jnp.zeros(n_state.shape, F32)
        m_state[...] = jnp.zeros(m_state.shape, F32)

    xpad[SUBLANES:SUBLANES + TB, :] = pm_ref[:, 0:DQK]
    acc = cb_ref[...] + cw_ref[QK_CONV - 1:QK_CONV, :] * pm_ref[:, 0:DQK]
    for j in range(QK_CONV - 1):
        d = QK_CONV - 1 - j
        acc = acc + cw_ref[j:j + 1, :] * xpad[SUBLANES - d:SUBLANES - d + TB, :]
    xpad[0:SUBLANES, :] = xpad[TB:TB + SUBLANES, :]
    qk = _silu(acc)
    q_s[...] = qk[:, 0:D_MLSTM]
    k_s[...] = qk[:, D_MLSTM:DQK] * (N ** -0.5)

    lane = lax.broadcasted_iota(jnp.int32, (1, LANES), 1)
    gc = pg_ref[...] + gbc_ref[...]
    gc = jnp.where(lane < H, gc, jnp.where(lane < 2 * H, _log_sigmoid(gc), 0.0))
    cum_c = _dot_sel_l(tri_ref[...], gc)
    gr = grow_ref[...] + gbr_ref[:, 0:1]
    sub = lax.broadcasted_iota(jnp.int32, (N_GATES, 1), 0)
    gr = jnp.where(sub < H, gr, _log_sigmoid(gr))
    cum_r = _dot_sel_r(gr, trit_ref[...])

    ti = lax.broadcasted_iota(jnp.int32, (L, L), 0)
    si = lax.broadcasted_iota(jnp.int32, (L, L), 1)
    causal = ti >= si

    NC = TB // L
    units = [(c, h) for c in range(NC) for h in range(H)]

    def gate_parts(c, h):
        rows = slice(c * L, (c + 1) * L)
        b_col = cum_c[rows, H + h:H + h + 1]
        ig_col = gc[rows, h:h + 1]
        b_row = cum_r[H + h:H + h + 1, rows]
        ig_row = gr[h:h + 1, rows]
        dmat = jnp.where(causal, b_col - b_row + ig_row, -jnp.inf)
        b_end = b_col[L - 1:L, :]
        gl_col = b_end - b_col + ig_col
        return dict(b_col=b_col, dmat=dmat, dmax=jnp.max(dmat, axis=-1, keepdims=True),
                    b_end=b_end, gl_col=gl_col, gl_max=jnp.max(gl_col, axis=0, keepdims=True))

    gp = {u: gate_parts(*u) for u in units}
    m_in = {}
    for h in range(H):
        m = m_state[h][:, 0:1]
        for c in range(NC):
            m_in[(c, h)] = m
            m = jnp.maximum(gp[(c, h)]["b_end"] + m, gp[(c, h)]["gl_max"])
        m_state[h] = jnp.broadcast_to(m, (1, LANES))
        m_in[(NC, h)] = m

    def load(c, h):
        rows = slice(c * L, (c + 1) * L)
        cols = slice(h * N, (h + 1) * N)
        return (q_s[rows, cols], k_s[rows, cols],
                pm_ref[rows, 2 * D_MLSTM + h * N:2 * D_MLSTM + (h + 1) * N])

    qkv = {u: load(*u) for u in units}
    inter = {u: gp[u]["b_col"] + m_in[u] for u in units}
    m_t = {u: jnp.maximum(inter[u], gp[u]["dmax"]) for u in units}
    dw = {u: jnp.exp(gp[u]["dmat"] - m_t[u]) for u in units}
    iw = {u: jnp.exp(inter[u] - m_t[u]) for u in units}
    s = {u: _dot_nt(qkv[u][0], qkv[u][1]) * dw[u] for u in units}
    sv = {u: _dot(s[u], qkv[u][2]) for u in units}
    ssum = {u: jnp.sum(s[u], axis=-1, keepdims=True) for u in units}
    ws = {(c, h): jnp.exp(gp[(c, h)]["gl_col"] - m_in[(c + 1, h)]) for c, h in units}
    dec = {(c, h): jnp.exp(gp[(c, h)]["b_end"] + m_in[(c, h)] - m_in[(c + 1, h)]) for c, h in units}
    kv = {u: _dot_tn(qkv[u][1], ws[u] * qkv[u][2]) for u in units}
    ksum = {u: jnp.sum(ws[u] * qkv[u][1], axis=0, keepdims=True) for u in units}
    c_in, n_in = {}, {}
    for h in range(H):
        cst, nst = c_state[h], n_state[h]
        for c in range(NC):
            c_in[(c, h)], n_in[(c, h)] = cst, nst
            cst = dec[(c, h)] * cst + kv[(c, h)]
            nst = dec[(c, h)] * nst + ksum[(c, h)]
        c_state[h], n_state[h] = cst, nst
    qc = {u: _dot(qkv[u][0], c_in[u]) for u in units}
    qn = {u: jnp.sum(qkv[u][0] * n_in[u], axis=-1, keepdims=True) for u in units}
    for c, h in units:
        u = (c, h)
        num = sv[u] + iw[u] * qc[u]
        den = ssum[u] + iw[u] * qn[u]
        h_s[c * L:(c + 1) * L, h * N:(h + 1) * N] = num / jnp.maximum(jnp.abs(den), jnp.exp(-m_t[u]))

    hh = h_s[...]
    o = pm_ref[:, 3 * D_MLSTM:4 * D_MLSTM]
    outs = []
    for h in range(H):
        y = hh[:, h * N:(h + 1) * N]
        mu = jnp.mean(y, axis=-1, keepdims=True)
        yc = y - mu
        var = jnp.mean(yc * yc, axis=-1, keepdims=True)
        outs.append(yc * lax.rsqrt(var + MHN_EPS))
    yn = jnp.concatenate(outs, axis=-1) * mhn_ref[...]
    o_ref[...] = (yn * jax.nn.sigmoid(o)).astype(o_ref.dtype)


def _mlstm(p_m, p_g, g_row, conv_w, conv_b, gb_col, gb_row, mhn_w):
    B, T, _ = p_m.shape
    TB, L = TB_MLSTM, MLSTM_CHUNK
    it = jnp.arange(TB)
    same_chunk = it[:, None] // L == it[None, :] // L
    tri = (same_chunk & (it[:, None] >= it[None, :])).astype(BF16)
    blk = lambda n: pl.BlockSpec((None, TB, n), lambda b, t: (b, t, 0))
    return pl.pallas_call(
        _mlstm_kernel,
        grid=(B, T // TB),
        in_specs=[blk(D_M), blk(LANES),
                  pl.BlockSpec((None, N_GATES, TB), lambda b, t: (b, 0, t)),
                  _const_spec((QK_CONV, 2 * D_MLSTM)), _const_spec((1, 2 * D_MLSTM)),
                  _const_spec((1, LANES)), _const_spec((N_GATES, LANES)),
                  _const_spec((1, D_MLSTM)), _const_spec((TB, TB)), _const_spec((TB, TB))],
        out_specs=blk(D_MLSTM),
        out_shape=jax.ShapeDtypeStruct((B, T, D_MLSTM), BF16),
        scratch_shapes=[pltpu.VMEM((TB + SUBLANES, 2 * D_MLSTM), F32),
                        pltpu.VMEM((TB, D_MLSTM), F32), pltpu.VMEM((TB, D_MLSTM), F32),
                        pltpu.VMEM((TB, D_MLSTM), F32),
                        pltpu.VMEM((MLSTM_HEADS, MLSTM_HEAD, MLSTM_HEAD), F32),
                        pltpu.VMEM((MLSTM_HEADS, 1, MLSTM_HEAD), F32),
                        pltpu.VMEM((MLSTM_HEADS, 1, LANES), F32)],
        compiler_params=pltpu.CompilerParams(dimension_semantics=("arbitrary", "arbitrary"),
                                             vmem_limit_bytes=VMEM_LIMIT),
        name="mlstm",
    )(p_m, p_g, g_row, conv_w, conv_b, gb_col, gb_row, mhn_w, tri, tri.T)


def _ffn_kernel(x_ref, yr_ref, ym_ref, mod_ref, wout_ref, n2g_ref, wup_ref, cw_ref, cb_ref,
                wdn_ref, nfg_ref, o_ref, upad_a, upad_l, tail_a, tail_l, acc_s):
    TM = TM_FFN
    t_idx = pl.program_id(1)

    @pl.when(t_idx == 0)
    def _():
        tail_a[...] = jnp.zeros(tail_a.shape, F32)
        tail_l[...] = jnp.zeros(tail_l.shape, F32)

    y = (jnp.dot(yr_ref[...], wout_ref[0:D_RWKV, :], preferred_element_type=F32)
         + jnp.dot(ym_ref[...], wout_ref[D_RWKV:2 * D_RWKV, :], preferred_element_type=F32))
    x1 = x_ref[...] + mod_ref[2:3, :] * y
    h = _rms_mod(x1, n2g_ref[...], mod_ref[3:4, :], mod_ref[4:5, :]).astype(BF16)

    def conv(u, upad, tail, col0):
        cols = slice(col0, col0 + FF_TILE)
        upad[0:SUBLANES, :] = tail[:, cols]
        upad[SUBLANES:SUBLANES + TM, :] = u
        tail[:, cols] = u[TM - SUBLANES:TM, :]
        out = cb_ref[:, cols] + cw_ref[FF_CONV - 1:FF_CONV, cols] * u
        for j in range(FF_CONV - 1):
            d = FF_CONV - 1 - j
            out = out + cw_ref[j:j + 1, cols] * upad[SUBLANES - d:SUBLANES - d + TM, :]
        return out

    acc_s[...] = jnp.zeros(acc_s.shape, F32)
    for j in range(D_FF // FF_TILE):
        ca = j * FF_TILE
        cl = D_FF + j * FF_TILE
        ua = jnp.dot(h, wup_ref[:, ca:ca + FF_TILE], preferred_element_type=F32)
        ul = jnp.dot(h, wup_ref[:, cl:cl + FF_TILE], preferred_element_type=F32)
        act = _silu(conv(ua, upad_a, tail_a, ca)) * conv(ul, upad_l, tail_l, cl)
        acc_s[...] += jnp.dot(act.astype(BF16), wdn_ref[ca:ca + FF_TILE, :],
                              preferred_element_type=F32)

    x2 = x1 + mod_ref[5:6, :] * acc_s[...]
    o_ref[...] = (x2 * lax.rsqrt(jnp.mean(x2 * x2, axis=-1, keepdims=True) + NORM_EPS)
                  * nfg_ref[...])


def _ffn(x, y_r, y_m, mod, w_out, norm2_g, w_up, conv_w, conv_b, w_down, normf_g):
    B, T, _ = x.shape
    TM = TM_FFN
    blk = lambda n: pl.BlockSpec((None, TM, n), lambda b, t: (b, t, 0))
    return pl.pallas_call(
        _ffn_kernel,
        grid=(B, T // TM),
        in_specs=[blk(D_MODEL), blk(D_RWKV), blk(D_MLSTM),
                  pl.BlockSpec((None, 6, D_MODEL), lambda b, t: (b, 0, 0)),
                  _const_spec((D_MODEL, D_MODEL)), _const_spec((1, D_MODEL)),
                  _const_spec((D_MODEL, 2 * D_FF)), _const_spec((FF_CONV, 2 * D_FF)),
                  _const_spec((1, 2 * D_FF)), _const_spec((D_FF, D_MODEL)),
                  _const_spec((1, D_MODEL))],
        out_specs=blk(D_MODEL),
        out_shape=jax.ShapeDtypeStruct((B, T, D_MODEL), F32),
        scratch_shapes=[pltpu.VMEM((TM + SUBLANES, FF_TILE), F32),
                        pltpu.VMEM((TM + SUBLANES, FF_TILE), F32),
                        pltpu.VMEM((SUBLANES, 2 * D_FF), F32),
                        pltpu.VMEM((SUBLANES, 2 * D_FF), F32),
                        pltpu.VMEM((TM, D_MODEL), F32)],
        compiler_params=pltpu.CompilerParams(dimension_semantics=("arbitrary", "arbitrary"),
                                             vmem_limit_bytes=VMEM_LIMIT),
        name="ffn",
    )(x, y_r, y_m, mod, w_out, norm2_g, w_up, conv_w, conv_b, w_down, normf_g)


def _pad_cols(a, n):
    return jnp.pad(a, ((0, 0), (0, n - a.shape[1])))


def _pad_rows(a, n):
    return jnp.pad(a, ((0, n - a.shape[0]), (0, 0)))


def kernel(x, c, w_ada, b_ada, norm1_g, norm2_g, normf_g, w_in, mu_rwkv, w_w2, w0, w_a2, a0, w_g2,
           k_k, k_a, r_k, lnx_w, lnx_b, conv_qk_w, conv_qk_b, i_bias, f_bias, mhn_w, w_out, w_up,
           conv_ff_w, conv_ff_b, w_down):
    assert w_ada.shape[0] == 1, "single-layer block"
    B = x.shape[0]
    row = lambda a: a.reshape(1, -1)
    R3 = 3 * D_RWKV
    d_rin = R3 + W_LORA + A_LORA + G_LORA

    wi = w_in[0]

    def rwkv_cols(a):
        return jnp.concatenate([a[:, 0:R3], _pad_cols(a[:, R3:R3 + W_LORA], LANES),
                                _pad_cols(a[:, R3 + W_LORA:R3 + W_LORA + A_LORA], LANES),
                                a[:, R3 + W_LORA + A_LORA:d_rin]], axis=1)

    w_gate = wi[:, d_rin + D_M:]
    w_all = jnp.concatenate([rwkv_cols(wi[:, :d_rin]), wi[:, d_rin:d_rin + D_M],
                             _pad_cols(w_gate, LANES)], axis=1).astype(BF16)
    w_gt = w_gate.T.astype(BF16)
    mu = rwkv_cols(row(mu_rwkv[0]))
    gates_b = jnp.concatenate([i_bias[0], f_bias[0]])
    gb_col = _pad_cols(row(gates_b), LANES)
    gb_row = jnp.broadcast_to(gates_b[:, None], (N_GATES, LANES))

    mod = _ada(c, w_ada[0], b_ada[0])
    mod = jnp.transpose(mod, (1, 0, 2))
    p_r, p_m, p_g, g_row = _inproj(x, mod, row(norm1_g[0]), w_all, w_gt)
    y_r = _rwkv(p_r, mu, _pad_rows(w_w2[0], LANES).astype(BF16), row(w0[0]),
                _pad_rows(w_a2[0], LANES).astype(BF16), row(a0[0]), w_g2[0].astype(BF16),
                row(k_k[0]), row(k_a[0]), row(r_k[0]), row(lnx_w[0]), row(lnx_b[0]))
    y_m = _mlstm(p_m, p_g, g_row, conv_qk_w[0], row(conv_qk_b[0]), gb_col, gb_row, row(mhn_w[0]))
    return _ffn(x, y_r, y_m, mod, w_out[0].astype(BF16), row(norm2_g[0]), w_up[0].astype(BF16),
                conv_ff_w[0], row(conv_ff_b[0]), w_down[0].astype(BF16), row(normf_g))
```

```python
import functools

import jax
import jax.numpy as jnp
from jax import lax
from jax.experimental import pallas as pl
from jax.experimental.pallas import tpu as pltpu

D_MODEL = 1024
D_RWKV = 512
RWKV_HEAD = 64
RWKV_HEADS = 8
RWKV_PAIRS = RWKV_HEADS // 2
W_LORA = 64
A_LORA = 64
G_LORA = 128
D_MLSTM = 512
MLSTM_HEADS = 4
MLSTM_HEAD = 128
QK_CONV = 4
MLSTM_CHUNK = 64
D_FF = 2816
FF_CONV = 3
NORM_EPS = 1e-6
LNX_EPS = 64e-5
MHN_EPS = 1e-6

LANES = 128
SUBLANES = 8
RWKV_CHUNK = 64
INV_BASE = 2
SEG_TILE = 256
RWKV_GROUP = 16
D_R = 3 * D_RWKV + 3 * LANES
D_M = 4 * D_MLSTM
D_P = D_R + D_M + LANES
N_GATES = 2 * MLSTM_HEADS

TM_IN = 512
TB_RWKV = 256
TB_MLSTM = 256
TM_FFN = 512
FF_TILE = 256
VMEM_LIMIT = 56 * 1024 * 1024

F32 = jnp.float32
BF16 = jnp.bfloat16


def _dot(a, b):
    return jnp.dot(a.astype(BF16), b.astype(BF16), preferred_element_type=F32)


def _dot_nt(a, b):
    return lax.dot_general(a.astype(BF16), b.astype(BF16), (((1,), (1,)), ((), ())),
                           preferred_element_type=F32)


def _dot_tn(a, b):
    return lax.dot_general(a.astype(BF16), b.astype(BF16), (((0,), (0,)), ((), ())),
                           preferred_element_type=F32)


def _head_sums(x, seg):
    hi = x.astype(BF16)
    lo = (x - hi.astype(F32)).astype(BF16)
    d = lambda t: jnp.dot(t, seg, preferred_element_type=F32)
    tiles = [d(hi[:, c:c + SEG_TILE]) + d(lo[:, c:c + SEG_TILE])
             for c in range(0, x.shape[1], SEG_TILE)]
    return jnp.concatenate(tiles, axis=1)


def _split3(x):
    hi = x.astype(BF16)
    r1 = x - hi.astype(F32)
    mid = r1.astype(BF16)
    lo = (r1 - mid.astype(F32)).astype(BF16)
    return hi, mid, lo


def _dot_sel_l(sel, x):
    hi, mid, lo = _split3(x)
    d = lambda t: jnp.dot(sel, t, preferred_element_type=F32)
    return d(hi) + (d(mid) + d(lo))


def _dot_sel_r(x, sel):
    hi, mid, lo = _split3(x)
    d = lambda t: jnp.dot(t, sel, preferred_element_type=F32)
    return d(hi) + (d(mid) + d(lo))


def _log_sigmoid(z):
    return jnp.minimum(z, 0.0) - jnp.log1p(jnp.exp(-jnp.abs(z)))


def _silu(z):
    return z * jax.nn.sigmoid(z)


def _shift_rows(u, tail, d):
    rolled = pltpu.roll(u, d, axis=0)
    sub = lax.broadcasted_iota(jnp.int32, (SUBLANES, u.shape[1]), 0)
    head = jnp.where(sub < d, pltpu.roll(tail, d, axis=0), rolled[0:SUBLANES])
    return jnp.concatenate([head, rolled[SUBLANES:]], axis=0)


def _const_spec(shape):
    nd = len(shape)
    return pl.BlockSpec(shape, lambda *_: (0,) * nd, pipeline_mode=pl.Buffered(1))


def _ada_kernel(c_ref, w_ref, b_ref, o_ref):
    s = _silu(c_ref[...])
    o_ref[0] = jnp.dot(s, w_ref[...], preferred_element_type=F32,
                       precision=lax.Precision.HIGHEST) + b_ref[0]


def _ada(c, w_ada, b_ada):
    B = c.shape[0]
    return pl.pallas_call(
        _ada_kernel,
        grid=(6,),
        in_specs=[pl.BlockSpec((B, D_MODEL), lambda j: (0, 0)),
                  pl.BlockSpec((D_MODEL, D_MODEL), lambda j: (0, j)),
                  pl.BlockSpec((1, 1, D_MODEL), lambda j: (j, 0, 0))],
        out_specs=pl.BlockSpec((1, B, D_MODEL), lambda j: (j, 0, 0)),
        out_shape=jax.ShapeDtypeStruct((6, B, D_MODEL), F32),
        compiler_params=pltpu.CompilerParams(dimension_semantics=("arbitrary",),
                                             vmem_limit_bytes=VMEM_LIMIT),
        name="ada",
    )(c, w_ada, b_ada.reshape(6, 1, D_MODEL))


def _rms_mod(x, g, shift, scale):
    y = x * lax.rsqrt(jnp.mean(x * x, axis=-1, keepdims=True) + NORM_EPS) * g
    return y * (1.0 + scale) + shift


def _inproj_kernel(x_ref, mod_ref, g_ref, w_ref, wgt_ref, pr_ref, pm_ref, pg_ref, grow_ref):
    h = _rms_mod(x_ref[...], g_ref[...], mod_ref[0:1, :], mod_ref[1:2, :]).astype(BF16)
    pr_ref[...] = jnp.dot(h, w_ref[:, 0:D_R], preferred_element_type=F32)
    pm_ref[...] = jnp.dot(h, w_ref[:, D_R:D_R + D_M], preferred_element_type=F32)
    pg_ref[...] = jnp.dot(h, w_ref[:, D_R + D_M:D_P], preferred_element_type=F32)
    grow_ref[...] = lax.dot_general(wgt_ref[...], h, (((1,), (1,)), ((), ())),
                                    preferred_element_type=F32)


def _inproj(x, mod, norm_g, w_all, w_gt):
    B, T, _ = x.shape
    return pl.pallas_call(
        _inproj_kernel,
        grid=(B, T // TM_IN),
        in_specs=[pl.BlockSpec((None, TM_IN, D_MODEL), lambda b, t: (b, t, 0)),
                  pl.BlockSpec((None, 6, D_MODEL), lambda b, t: (b, 0, 0)),
                  _const_spec((1, D_MODEL)),
                  _const_spec((D_MODEL, D_P)),
                  _const_spec((N_GATES, D_MODEL))],
        out_specs=[pl.BlockSpec((None, TM_IN, D_R), lambda b, t: (b, t, 0)),
                   pl.BlockSpec((None, TM_IN, D_M), lambda b, t: (b, t, 0)),
                   pl.BlockSpec((None, TM_IN, LANES), lambda b, t: (b, t, 0)),
                   pl.BlockSpec((None, N_GATES, TM_IN), lambda b, t: (b, 0, t))],
        out_shape=[jax.ShapeDtypeStruct((B, T, D_R), F32),
                   jax.ShapeDtypeStruct((B, T, D_M), F32),
                   jax.ShapeDtypeStruct((B, T, LANES), F32),
                   jax.ShapeDtypeStruct((B, N_GATES, T), F32)],
        compiler_params=pltpu.CompilerParams(dimension_semantics=("arbitrary", "arbitrary"),
                                             vmem_limit_bytes=VMEM_LIMIT),
        name="inproj",
    )(x, mod, norm_g, w_all, w_gt)


def _rwkv_kernel(p_ref, mu_ref, ww2_ref, w0_ref, wa2_ref, a0_ref, wg2_ref, kk_ref, ka_ref, rk_ref,
                 lnw_ref, lnb_ref, seg_ref, tri_ref, o_ref,
                 ppad, rt_s, at_s, bt_s, kt_s, bh_s, kh_s, v_s, pt_s, y_s, state,
                 rp_s, yp_s, gm_s, tp_s):
    TB, L = TB_RWKV, RWKV_CHUNK
    t_idx = pl.program_id(1)

    @pl.when(t_idx == 0)
    def _():
        ppad[0:SUBLANES, :] = jnp.zeros((SUBLANES, D_R), F32)
        state[...] = jnp.zeros(state.shape, F32)

    ppad[SUBLANES:SUBLANES + TB, :] = p_ref[...]
    p = p_ref[...]
    prev = ppad[SUBLANES - 1:SUBLANES - 1 + TB, :]
    xs = p + (prev - p) * mu_ref[...]
    ppad[0:SUBLANES, :] = ppad[TB:TB + SUBLANES, :]

    r = xs[:, 0:D_RWKV]
    k = xs[:, D_RWKV:2 * D_RWKV]
    v = xs[:, 2 * D_RWKV:3 * D_RWKV]
    xw = xs[:, 3 * D_RWKV:3 * D_RWKV + LANES]
    xa = xs[:, 3 * D_RWKV + LANES:3 * D_RWKV + 2 * LANES]
    xg = xs[:, 3 * D_RWKV + 2 * LANES:D_R]

    w = _log_sigmoid(w0_ref[...] + _dot(jnp.tanh(xw), ww2_ref[...])) - 0.5
    a = jax.nn.sigmoid(a0_ref[...] + _dot(xa, wa2_ref[...]))
    g = _dot(jax.nn.sigmoid(xg), wg2_ref[...])
    seg = seg_ref[...]
    kk = k * kk_ref[...]
    kk = kk / jnp.maximum(jnp.sqrt(_head_sums(kk * kk, seg)), 1e-12)
    k2 = k * (1.0 + (a - 1.0) * ka_ref[...])
    bonus = _head_sums(r * k2 * rk_ref[...], seg)

    lw = -jnp.exp(w)
    cum = _dot_sel_l(tri_ref[...], lw)
    tot = jnp.concatenate([jnp.broadcast_to(cum[c * L + L - 1:c * L + L, :], (L, D_RWKV))
                           for c in range(TB // L)], axis=0)
    e_neg = jnp.exp(-cum)
    e_end = jnp.exp(tot - cum)
    b = kk * a
    rt_s[...] = r * jnp.exp(cum)
    at_s[...] = -kk * jnp.exp(cum - lw)
    bt_s[...] = b * e_neg
    kt_s[...] = k2 * e_neg
    bh_s[...] = b * e_end
    kh_s[...] = k2 * e_end
    v_s[...] = v
    pt_s[...] = jnp.exp(tot)

    lane = lax.broadcasted_iota(jnp.int32, (1, LANES), 1)
    m0 = (lane < RWKV_HEAD).astype(F32)
    m1 = 1.0 - m0
    ri = lax.broadcasted_iota(jnp.int32, (2 * L, 2 * L), 0)
    ci = lax.broadcasted_iota(jnp.int32, (2 * L, 2 * L), 1)
    same = (ri // L) == (ci // L)
    tril_s = (same & (ri > ci)).astype(F32)
    tril_i = (same & (ri >= ci)).astype(F32)
    eye = (ri == ci).astype(F32)
    blockdiag = ((ri // RWKV_HEAD) == (ci // RWKV_HEAD)).astype(F32)
    base_mask = ((ri // INV_BASE) == (ci // INV_BASE)).astype(F32)
    level_masks = []
    size = INV_BASE
    while size < L:
        level_masks.append((((ri // (2 * size)) == (ci // (2 * size)))
                            & ((ri // size) != (ci // size))).astype(F32))
        size *= 2

    def stack(t):
        return jnp.concatenate([t * m0, t * m1], axis=0)

    def chunk_tables(unit):
        c, pr = unit
        rows = slice(c * L, (c + 1) * L)
        cols = slice(pr * LANES, (pr + 1) * LANES)
        return dict(a=stack(at_s[rows, cols]), r=stack(rt_s[rows, cols]), b=stack(bt_s[rows, cols]),
                    k=stack(kt_s[rows, cols]), v=stack(v_s[rows, cols]), bh=stack(bh_s[rows, cols]),
                    kh=stack(kh_s[rows, cols]), ptot=pt_s[c * L:c * L + 1, cols])

    units = [(c, pr) for c in range(TB // L) for pr in range(RWKV_PAIRS)]
    for g0 in range(0, len(units), RWKV_GROUP):
        group = units[g0:g0 + RWKV_GROUP]
        tabs = [chunk_tables(u) for u in group]
        prods = [_dot_nt(jnp.concatenate([t["a"], t["r"]], axis=0),
                         jnp.concatenate([t["b"], t["k"]], axis=0)) for t in tabs]
        n_ab = [p[0:2 * L, 0:2 * L] * tril_s for p in prods]
        a_ak = [p[0:2 * L, 2 * L:4 * L] * tril_s for p in prods]
        a_rb = [p[2 * L:4 * L, 0:2 * L] * tril_i for p in prods]
        a_rk = [p[2 * L:4 * L, 2 * L:4 * L] * tril_i for p in prods]
        inv = [eye + n * base_mask for n in n_ab]
        for lvl_mask in level_masks:
            cm = [_dot(n * lvl_mask, m) for n, m in zip(n_ab, inv)]
            inv = [m + _dot(m, x) for m, x in zip(inv, cm)]
        av = [_dot(jnp.concatenate([x, y], axis=0), t["v"]) for x, y, t in zip(a_ak, a_rk, tabs)]
        w = [_dot(m, jnp.concatenate([t["a"], x[0:2 * L]], axis=1)) for m, t, x in zip(inv, tabs, av)]
        q = [_dot(x, y) for x, y in zip(a_rb, w)]
        tn = [_dot_tn(t["bh"], y) for t, y in zip(tabs, w)]
        kv = [_dot_tn(t["kh"], t["v"]) for t in tabs]
        for i, (c, pr) in enumerate(group):
            u = c * RWKV_PAIRS + pr
            rp_s[u] = tabs[i]["r"] + q[i][:, 0:LANES]
            yp_s[u] = q[i][:, LANES:2 * LANES] + av[i][2 * L:4 * L]
            gm_s[u] = eye * tabs[i]["ptot"] + tn[i][:, 0:LANES]
            tp_s[u] = tn[i][:, LANES:2 * LANES] + kv[i]

    for c in range(TB // L):
        us = [c * RWKV_PAIRS + pr for pr in range(RWKV_PAIRS)]
        sts = [state[pr] for pr in range(RWKV_PAIRS)]
        both = [_dot(jnp.concatenate([rp_s[u], gm_s[u]], axis=0), st) for u, st in zip(us, sts)]
        for pr, (u, z) in enumerate(zip(us, both)):
            y_st = z[0:2 * L] + yp_s[u]
            state[pr] = (z[2 * L:4 * L] + tp_s[u]) * blockdiag
            y_s[c * L:(c + 1) * L, pr * LANES:(pr + 1) * LANES] = y_st[0:L, :] + y_st[L:2 * L, :]

    y = y_s[...]
    mean = _head_sums(y, seg) * (1.0 / RWKV_HEAD)
    yc = y - mean
    var = _head_sums(yc * yc, seg) * (1.0 / RWKV_HEAD)
    yn = yc * lax.rsqrt(var + LNX_EPS) * lnw_ref[...] + lnb_ref[...]
    o_ref[...] = ((yn + bonus * v) * g).astype(o_ref.dtype)


def _rwkv(p_r, mu, ww2, w0, wa2, a0, wg2, k_k, k_a, r_k, lnx_w, lnx_b):
    B, T, _ = p_r.shape
    TB, L = TB_RWKV, RWKV_CHUNK
    iseg = jnp.arange(SEG_TILE)
    seg = (iseg[:, None] // RWKV_HEAD == iseg[None, :] // RWKV_HEAD).astype(BF16)
    it = jnp.arange(TB)
    same_chunk = it[:, None] // L == it[None, :] // L
    tri = (same_chunk & (it[:, None] >= it[None, :])).astype(BF16)
    row = lambda n: _const_spec((1, n))
    blk = lambda n: pl.BlockSpec((None, TB, n), lambda b, t: (b, t, 0))
    scr = lambda: pltpu.VMEM((TB, D_RWKV), F32)
    return pl.pallas_call(
        _rwkv_kernel,
        grid=(B, T // TB),
        in_specs=[blk(D_R), row(D_R), _const_spec((LANES, D_RWKV)), row(D_RWKV),
                  _const_spec((LANES, D_RWKV)), row(D_RWKV), _const_spec((G_LORA, D_RWKV)),
                  row(D_RWKV), row(D_RWKV), row(D_RWKV), row(D_RWKV), row(D_RWKV),
                  _const_spec((SEG_TILE, SEG_TILE)), _const_spec((TB, TB))],
        out_specs=blk(D_RWKV),
        out_shape=jax.ShapeDtypeStruct((B, T, D_RWKV), BF16),
        scratch_shapes=[pltpu.VMEM((TB + SUBLANES, D_R), F32)] + [scr() for _ in range(9)]
                       + [pltpu.VMEM((RWKV_PAIRS, LANES, LANES), F32)]
                       + [pltpu.VMEM(((TB // L) * RWKV_PAIRS, 2 * L, LANES), F32) for _ in range(4)],
        compiler_params=pltpu.CompilerParams(dimension_semantics=("arbitrary", "arbitrary"),
                                             vmem_limit_bytes=VMEM_LIMIT),
        name="rwkv",
    )(p_r, mu, ww2, w0, wa2, a0, wg2, k_k, k_a, r_k, lnx_w, lnx_b, seg, tri)


def _chunk_cummax(x, pos):
    d = 1
    while d < MLSTM_CHUNK:
        x = jnp.maximum(x, jnp.where(pos >= d, pltpu.roll(x, d, axis=0), -jnp.inf))
        d *= 2
    return x


def _mlstm_kernel(pm_ref, pg_ref, grow_ref, cw_ref, cb_ref, gbc_ref, gbr_ref, mhn_ref, tri_ref,
                  trit_ref, rep_ref, seg_ref, o_ref, tail, q_s, k_s, h_s, cn_state, m_state):
    TB, L, H, N = TB_MLSTM, MLSTM_CHUNK, MLSTM_HEADS, MLSTM_HEAD
    DQK = 2 * D_MLSTM
    NC = TB // L
    t_idx = pl.program_id(1)

    @pl.when(t_idx == 0)
    def _():
        tail[...] = jnp.zeros(tail.shape, F32)
        cn_state[...] = jnp.zeros(cn_state.shape, F32)
        m_state[...] = jnp.zeros(m_state.shape, F32)

    x = pm_ref[:, 0:DQK]
    prev_tail = tail[...]
    tail[...] = x[TB - SUBLANES:TB, :]
    acc = cb_ref[...] + cw_ref[QK_CONV - 1:QK_CONV, :] * x
    for j in range(QK_CONV - 1):
        acc = acc + cw_ref[j:j + 1, :] * _shift_rows(x, prev_tail, QK_CONV - 1 - j)
    qk = _silu(acc)
    q_s[...] = qk[:, 0:D_MLSTM]
    k_s[...] = qk[:, D_MLSTM:DQK] * (N ** -0.5)

    lane = lax.broadcasted_iota(jnp.int32, (1, LANES), 1)
    gc = pg_ref[...] + gbc_ref[...]
    gc = jnp.where(lane < H, gc, jnp.where(lane < 2 * H, _log_sigmoid(gc), 0.0))
    rep = _dot_sel_r(gc, rep_ref[...])
    ig_c = rep[:, 0:H * LANES]
    b_c = _dot_sel_l(tri_ref[...], rep[:, H * LANES:2 * H * LANES])
    g_c = ig_c - b_c
    gr = grow_ref[...] + gbr_ref[:, 0:1]
    sub = lax.broadcasted_iota(jnp.int32, (N_GATES, 1), 0)
    gr = jnp.where(sub < H, gr, _log_sigmoid(gr))
    g_r = gr[0:H, :] - _dot_sel_r(gr, trit_ref[...])[H:2 * H, :]

    pos = lax.broadcasted_iota(jnp.int32, (TB, LANES), 0) % L
    ti = lax.broadcasted_iota(jnp.int32, (L, L), 0)
    si = lax.broadcasted_iota(jnp.int32, (L, L), 1)
    causal = ti >= si
    ones_v = jnp.ones((L, N), F32)

    a_in, m_in = {}, {}
    for h in range(H):
        cols = slice(h * LANES, (h + 1) * LANES)
        cmax = _chunk_cummax(g_c[:, cols], pos)
        m = m_state[h]
        for c in range(NC):
            a = jnp.maximum(m, cmax[c * L:(c + 1) * L, :])
            a_in[(c, h)], m_in[(c, h)] = a, m
            m = b_c[c * L + L - 1:c * L + L, cols] + a[L - 1:L, :]
        m_state[h] = m

    units = [(c, h) for c in range(NC) for h in range(H)]

    def load(c, h):
        rows = slice(c * L, (c + 1) * L)
        cols = slice(h * N, (h + 1) * N)
        return (q_s[rows, cols], k_s[rows, cols],
                pm_ref[rows, 2 * D_MLSTM + h * N:2 * D_MLSTM + (h + 1) * N])

    qkv = {u: load(*u) for u in units}
    dw = {(c, h): jnp.exp(jnp.where(causal, g_r[h:h + 1, c * L:(c + 1) * L] - a_in[(c, h)][:, 0:L],
                                    -jnp.inf)) for c, h in units}
    iw = {u: jnp.exp(m_in[u] - a_in[u]) for u in units}
    s = {u: _dot_nt(qkv[u][0], qkv[u][1]) * dw[u] for u in units}
    sv = {u: _dot(s[u], jnp.concatenate([qkv[u][2], ones_v], axis=1)) for u in units}
    a_end = {u: a_in[u][L - 1:L, :] for u in units}
    ws = {(c, h): jnp.exp(g_c[c * L:(c + 1) * L, h * LANES:(h + 1) * LANES] - a_end[(c, h)])
          for c, h in units}
    dec = {u: jnp.exp(m_in[u] - a_end[u]) for u in units}
    kv = {u: _dot_tn(qkv[u][1], jnp.concatenate([ws[u] * qkv[u][2], ws[u]], axis=1)) for u in units}
    cn_in = {}
    for h in range(H):
        cn = cn_state[h]
        for c in range(NC):
            cn_in[(c, h)] = cn
            cn = jnp.concatenate([dec[(c, h)], dec[(c, h)]], axis=1) * cn + kv[(c, h)]
        cn_state[h] = cn
    qc = {u: _dot(qkv[u][0], cn_in[u]) for u in units}
    for c, h in units:
        u = (c, h)
        num = sv[u][:, 0:N] + iw[u] * qc[u][:, 0:N]
        den = sv[u][:, N:2 * N] + iw[u] * qc[u][:, N:2 * N]
        m_t = b_c[c * L:(c + 1) * L, h * LANES:(h + 1) * LANES] + a_in[u]
        h_s[c * L:(c + 1) * L, h * N:(h + 1) * N] = num / jnp.maximum(jnp.abs(den), jnp.exp(-m_t))

    hh = h_s[...]
    seg = seg_ref[...]
    mean = _head_sums(hh, seg) * (1.0 / N)
    yc = hh - mean
    var = _head_sums(yc * yc, seg) * (1.0 / N)
    yn = yc * lax.rsqrt(var + MHN_EPS) * mhn_ref[...]
    o_ref[...] = (yn * jax.nn.sigmoid(pm_ref[:, 3 * D_MLSTM:4 * D_MLSTM])).astype(o_ref.dtype)


def _mlstm(p_m, p_g, g_row, conv_w, conv_b, gb_col, gb_row, mhn_w):
    B, T, _ = p_m.shape
    TB, L = TB_MLSTM, MLSTM_CHUNK
    it = jnp.arange(TB)
    same_chunk = it[:, None] // L == it[None, :] // L
    tri = (same_chunk & (it[:, None] >= it[None, :])).astype(BF16)
    rep = (jnp.arange(LANES)[:, None] == jnp.arange(N_GATES * LANES)[None, :] // LANES).astype(BF16)
    iseg = jnp.arange(SEG_TILE)
    seg = (iseg[:, None] // MLSTM_HEAD == iseg[None, :] // MLSTM_HEAD).astype(BF16)
    blk = lambda n: pl.BlockSpec((None, TB, n), lambda b, t: (b, t, 0))
    return pl.pallas_call(
        _mlstm_kernel,
        grid=(B, T // TB),
        in_specs=[blk(D_M), blk(LANES),
                  pl.BlockSpec((None, N_GATES, TB), lambda b, t: (b, 0, t)),
                  _const_spec((QK_CONV, 2 * D_MLSTM)), _const_spec((1, 2 * D_MLSTM)),
                  _const_spec((1, LANES)), _const_spec((N_GATES, LANES)),
                  _const_spec((1, D_MLSTM)), _const_spec((TB, TB)), _const_spec((TB, TB)),
                  _const_spec((LANES, N_GATES * LANES)), _const_spec((SEG_TILE, SEG_TILE))],
        out_specs=blk(D_MLSTM),
        out_shape=jax.ShapeDtypeStruct((B, T, D_MLSTM), BF16),
        scratch_shapes=[pltpu.VMEM((SUBLANES, 2 * D_MLSTM), F32),
                        pltpu.VMEM((TB, D_MLSTM), F32), pltpu.VMEM((TB, D_MLSTM), F32),
                        pltpu.VMEM((TB, D_MLSTM), F32),
                        pltpu.VMEM((MLSTM_HEADS, MLSTM_HEAD, 2 * MLSTM_HEAD), F32),
                        pltpu.VMEM((MLSTM_HEADS, 1, LANES), F32)],
        compiler_params=pltpu.CompilerParams(dimension_semantics=("arbitrary", "arbitrary"),
                                             vmem_limit_bytes=VMEM_LIMIT),
        name="mlstm",
    )(p_m, p_g, g_row, conv_w, conv_b, gb_col, gb_row, mhn_w, tri, tri.T, rep, seg)


def _ffn_kernel(x_ref, yr_ref, ym_ref, mod_ref, wout_ref, n2g_ref, wup_ref, cw_ref, cb_ref,
                wdn_ref, nfg_ref, o_ref, tail, act_s):
    TM = TM_FFN
    t_idx = pl.program_id(1)

    @pl.when(t_idx == 0)
    def _():
        tail[...] = jnp.zeros(tail.shape, F32)

    y = (jnp.dot(yr_ref[...], wout_ref[0:D_RWKV, :], preferred_element_type=F32)
         + jnp.dot(ym_ref[...], wout_ref[D_RWKV:2 * D_RWKV, :], preferred_element_type=F32))
    x1 = x_ref[...] + mod_ref[2:3, :] * y
    h = _rms_mod(x1, n2g_ref[...], mod_ref[3:4, :], mod_ref[4:5, :]).astype(BF16)

    def conv(u, col0):
        cols = slice(col0, col0 + FF_TILE)
        prev_tail = tail[:, cols]
        tail[:, cols] = u[TM - SUBLANES:TM, :]
        out = cb_ref[:, cols] + cw_ref[FF_CONV - 1:FF_CONV, cols] * u
        for j in range(FF_CONV - 1):
            out = out + cw_ref[j:j + 1, cols] * _shift_rows(u, prev_tail, FF_CONV - 1 - j)
        return out

    for j in range(D_FF // FF_TILE):
        ca = j * FF_TILE
        cl = D_FF + j * FF_TILE
        ua = jnp.dot(h, wup_ref[:, ca:ca + FF_TILE], preferred_element_type=F32)
        ul = jnp.dot(h, wup_ref[:, cl:cl + FF_TILE], preferred_element_type=F32)
        act_s[:, ca:ca + FF_TILE] = (_silu(conv(ua, ca)) * conv(ul, cl)).astype(BF16)

    ffn = jnp.dot(act_s[...], wdn_ref[...], preferred_element_type=F32)
    x2 = x1 + mod_ref[5:6, :] * ffn
    o_ref[...] = (x2 * lax.rsqrt(jnp.mean(x2 * x2, axis=-1, keepdims=True) + NORM_EPS)
                  * nfg_ref[...])


def _ffn(x, y_r, y_m, mod, w_out, norm2_g, w_up, conv_w, conv_b, w_down, normf_g):
    B, T, _ = x.shape
    TM = TM_FFN
    blk = lambda n: pl.BlockSpec((None, TM, n), lambda b, t: (b, t, 0))
    return pl.pallas_call(
        _ffn_kernel,
        grid=(B, T // TM),
        in_specs=[blk(D_MODEL), blk(D_RWKV), blk(D_MLSTM),
                  pl.BlockSpec((None, 6, D_MODEL), lambda b, t: (b, 0, 0)),
                  _const_spec((D_MODEL, D_MODEL)), _const_spec((1, D_MODEL)),
                  _const_spec((D_MODEL, 2 * D_FF)), _const_spec((FF_CONV, 2 * D_FF)),
                  _const_spec((1, 2 * D_FF)), _const_spec((D_FF, D_MODEL)),
                  _const_spec((1, D_MODEL))],
        out_specs=blk(D_MODEL),
        out_shape=jax.ShapeDtypeStruct((B, T, D_MODEL), F32),
        scratch_shapes=[pltpu.VMEM((SUBLANES, 2 * D_FF), F32),
                        pltpu.VMEM((TM, D_FF), BF16)],
        compiler_params=pltpu.CompilerParams(dimension_semantics=("arbitrary", "arbitrary"),
                                             vmem_limit_bytes=VMEM_LIMIT),
        name="ffn",
    )(x, y_r, y_m, mod, w_out, norm2_g, w_up, conv_w, conv_b, w_down, normf_g)


def _pad_cols(a, n):
    return jnp.pad(a, ((0, 0), (0, n - a.shape[1])))


def _pad_rows(a, n):
    return jnp.pad(a, ((0, n - a.shape[0]), (0, 0)))


def kernel(x, c, w_ada, b_ada, norm1_g, norm2_g, normf_g, w_in, mu_rwkv, w_w2, w0, w_a2, a0, w_g2,
           k_k, k_a, r_k, lnx_w, lnx_b, conv_qk_w, conv_qk_b, i_bias, f_bias, mhn_w, w_out, w_up,
           conv_ff_w, conv_ff_b, w_down):
    assert w_ada.shape[0] == 1, "single-layer block"
    B = x.shape[0]
    row = lambda a: a.reshape(1, -1)
    R3 = 3 * D_RWKV
    d_rin = R3 + W_LORA + A_LORA + G_LORA

    wi = w_in[0]

    def rwkv_cols(a):
        return jnp.concatenate([a[:, 0:R3], _pad_cols(a[:, R3:R3 + W_LORA], LANES),
                                _pad_cols(a[:, R3 + W_LORA:R3 + W_LORA + A_LORA], LANES),
                                a[:, R3 + W_LORA + A_LORA:d_rin]], axis=1)

    w_gate = wi[:, d_rin + D_M:]
    w_all = jnp.concatenate([rwkv_cols(wi[:, :d_rin]), wi[:, d_rin:d_rin + D_M],
                             _pad_cols(w_gate, LANES)], axis=1).astype(BF16)
    w_gt = w_gate.T.astype(BF16)
    mu = rwkv_cols(row(mu_rwkv[0]))
    gates_b = jnp.concatenate([i_bias[0], f_bias[0]])
    gb_col = _pad_cols(row(gates_b), LANES)
    gb_row = jnp.broadcast_to(gates_b[:, None], (N_GATES, LANES))

    mod = _ada(c, w_ada[0], b_ada[0])
    mod = jnp.transpose(mod, (1, 0, 2))
    p_r, p_m, p_g, g_row = _inproj(x, mod, row(norm1_g[0]), w_all, w_gt)
    y_r = _rwkv(p_r, mu, _pad_rows(w_w2[0], LANES).astype(BF16), row(w0[0]),
                _pad_rows(w_a2[0], LANES).astype(BF16), row(a0[0]), w_g2[0].astype(BF16),
                row(k_k[0]), row(k_a[0]), row(r_k[0]), row(lnx_w[0]), row(lnx_b[0]))
    y_m = _mlstm(p_m, p_g, g_row, conv_qk_w[0], row(conv_qk_b[0]), gb_col, gb_row, row(mhn_w[0]))
    return _ffn(x, y_r, y_m, mod, w_out[0].astype(BF16), row(norm2_g[0]), w_up[0].astype(BF16),
                conv_ff_w[0], row(conv_ff_b[0]), w_down[0].astype(BF16), row(normf_g))
```

```python
import jax
import jax.numpy as jnp
from jax import lax
from jax.experimental import pallas as pl
from jax.experimental.pallas import tpu as pltpu

D_MODEL = 1024
D_RWKV = 512
RWKV_HEAD = 64
RWKV_HEADS = 8
RWKV_PAIRS = RWKV_HEADS // 2
W_LORA = 64
A_LORA = 64
G_LORA = 128
D_MLSTM = 512
MLSTM_HEADS = 4
MLSTM_HEAD = 128
QK_CONV = 4
MLSTM_CHUNK = 64
D_FF = 2816
FF_CONV = 3
NORM_EPS = 1e-6
LNX_EPS = 64e-5
MHN_EPS = 1e-6

LANES = 128
SUBLANES = 8
RWKV_CHUNK = 64
INV_BASE = 2
SEG_TILE = 256
RWKV_SUB = 256
D_R = 3 * D_RWKV + 3 * LANES
D_M = 4 * D_MLSTM
D_P = D_R + D_M + LANES
N_GATES = 2 * MLSTM_HEADS

TM_IN = 512
TB_RWKV = 256
TB_MLSTM = 256
TM_FFN = 512
FF_TILE = 256
VMEM_LIMIT = 56 * 1024 * 1024

F32 = jnp.float32
BF16 = jnp.bfloat16


def _dot(a, b):
    return jnp.dot(a.astype(BF16), b.astype(BF16), preferred_element_type=F32)


def _dot_nt(a, b):
    return lax.dot_general(a.astype(BF16), b.astype(BF16), (((1,), (1,)), ((), ())),
                           preferred_element_type=F32)


def _dot_tn(a, b):
    return lax.dot_general(a.astype(BF16), b.astype(BF16), (((0,), (0,)), ((), ())),
                           preferred_element_type=F32)


def _head_sums(x, seg):
    hi = x.astype(BF16)
    lo = (x - hi.astype(F32)).astype(BF16)
    d = lambda t: jnp.dot(t, seg, preferred_element_type=F32)
    tiles = [d(hi[:, c:c + SEG_TILE]) + d(lo[:, c:c + SEG_TILE])
             for c in range(0, x.shape[1], SEG_TILE)]
    return jnp.concatenate(tiles, axis=1)


def _split3(x):
    hi = x.astype(BF16)
    r1 = x - hi.astype(F32)
    mid = r1.astype(BF16)
    lo = (r1 - mid.astype(F32)).astype(BF16)
    return hi, mid, lo


def _dot_sel_l(sel, x):
    hi, mid, lo = _split3(x)
    d = lambda t: jnp.dot(sel, t, preferred_element_type=F32)
    return d(hi) + (d(mid) + d(lo))


def _dot_sel_r(x, sel):
    hi, mid, lo = _split3(x)
    d = lambda t: jnp.dot(t, sel, preferred_element_type=F32)
    return d(hi) + (d(mid) + d(lo))


def _log_sigmoid(z):
    return jnp.minimum(z, 0.0) - jnp.log(1.0 + jnp.exp(-jnp.abs(z)))


def _silu(z):
    return z * jax.nn.sigmoid(z)


def _shift_rows(u, tail, d):
    rolled = pltpu.roll(u, d, axis=0)
    sub = lax.broadcasted_iota(jnp.int32, (SUBLANES, u.shape[1]), 0)
    head = jnp.where(sub < d, pltpu.roll(tail, d, axis=0), rolled[0:SUBLANES])
    return jnp.concatenate([head, rolled[SUBLANES:]], axis=0)


def _const_spec(shape):
    nd = len(shape)
    return pl.BlockSpec(shape, lambda *_: (0,) * nd, pipeline_mode=pl.Buffered(1))


def _ada_kernel(c_ref, w_ref, b_ref, o_ref):
    s = _silu(c_ref[...])
    o_ref[0] = jnp.dot(s, w_ref[...], preferred_element_type=F32,
                       precision=lax.Precision.HIGHEST) + b_ref[...]


def _ada(c, w_ada, b_ada):
    B = c.shape[0]
    return pl.pallas_call(
        _ada_kernel,
        grid=(6,),
        in_specs=[pl.BlockSpec((B, D_MODEL), lambda j: (0, 0)),
                  pl.BlockSpec((None, D_MODEL, D_MODEL), lambda j: (0, 0, j)),
                  pl.BlockSpec((1, D_MODEL), lambda j: (0, j))],
        out_specs=pl.BlockSpec((1, B, D_MODEL), lambda j: (j, 0, 0)),
        out_shape=jax.ShapeDtypeStruct((6, B, D_MODEL), F32),
        compiler_params=pltpu.CompilerParams(dimension_semantics=("arbitrary",),
                                             vmem_limit_bytes=VMEM_LIMIT),
        name="ada",
    )(c, w_ada, b_ada)


def _rms_mod(x, g, shift, scale):
    y = x * lax.rsqrt(jnp.mean(x * x, axis=-1, keepdims=True) + NORM_EPS) * g
    return y * (1.0 + scale) + shift


def _inproj_kernel(x_ref, mod_ref, g_ref, w_ref, pr_ref, pm_ref, pg_ref, grow_ref):
    h = _rms_mod(x_ref[...], g_ref[...], mod_ref[0:1, :], mod_ref[1:2, :]).astype(BF16)
    pr_ref[...] = jnp.dot(h, w_ref[:, 0:D_R], preferred_element_type=F32)
    pm_ref[...] = jnp.dot(h, w_ref[:, D_R:D_R + D_M], preferred_element_type=F32)
    pg = jnp.dot(h, w_ref[:, D_R + D_M:D_P], preferred_element_type=F32)
    pg_ref[...] = pg
    grow_ref[...] = pg.T[0:N_GATES, :]


def _inproj(x, mod, norm_g, w_all):
    B, T, _ = x.shape
    return pl.pallas_call(
        _inproj_kernel,
        grid=(B, T // TM_IN),
        in_specs=[pl.BlockSpec((None, TM_IN, D_MODEL), lambda b, t: (b, t, 0)),
                  pl.BlockSpec((None, 6, D_MODEL), lambda b, t: (b, 0, 0)),
                  _const_spec((1, D_MODEL)),
                  _const_spec((D_MODEL, D_P))],
        out_specs=[pl.BlockSpec((None, TM_IN, D_R), lambda b, t: (b, t, 0)),
                   pl.BlockSpec((None, TM_IN, D_M), lambda b, t: (b, t, 0)),
                   pl.BlockSpec((None, TM_IN, LANES), lambda b, t: (b, t, 0)),
                   pl.BlockSpec((None, N_GATES, TM_IN), lambda b, t: (b, 0, t))],
        out_shape=[jax.ShapeDtypeStruct((B, T, D_R), F32),
                   jax.ShapeDtypeStruct((B, T, D_M), F32),
                   jax.ShapeDtypeStruct((B, T, LANES), F32),
                   jax.ShapeDtypeStruct((B, N_GATES, T), F32)],
        compiler_params=pltpu.CompilerParams(dimension_semantics=("arbitrary", "arbitrary"),
                                             vmem_limit_bytes=VMEM_LIMIT),
        name="inproj",
    )(x, mod, norm_g, w_all)


def _rwkv_kernel(p_ref, mu_ref, ww2_ref, w0_ref, wa2_ref, a0_ref, wg2_ref, kk_ref, ka_ref, rk_ref,
                 lnw_ref, lnb_ref, seg_ref, tri_ref, o_ref,
                 tail, rt_s, at_s, bt_s, kt_s, bh_s, kh_s, v_s, pt_s, state):
    TB, L, SUB = TB_RWKV, RWKV_CHUNK, RWKV_SUB
    t_idx = pl.program_id(1)

    @pl.when(t_idx == 0)
    def _():
        tail[...] = jnp.zeros(tail.shape, F32)
        state[...] = jnp.zeros(state.shape, F32)

    lane = lax.broadcasted_iota(jnp.int32, (1, LANES), 1)
    m0 = (lane < RWKV_HEAD).astype(F32)
    m1 = 1.0 - m0
    m0h, m1h = m0.astype(BF16), m1.astype(BF16)
    ri = lax.broadcasted_iota(jnp.int32, (2 * L, 2 * L), 0)
    ci = lax.broadcasted_iota(jnp.int32, (2 * L, 2 * L), 1)
    same = (ri // L) == (ci // L)
    tril_s = (same & (ri > ci)).astype(F32)
    tril_i = (same & (ri >= ci)).astype(F32)
    tril_sh, tril_ih = tril_s.astype(BF16), tril_i.astype(BF16)
    eye = (ri == ci).astype(F32)
    blockdiag = ((ri // RWKV_HEAD) == (ci // RWKV_HEAD)).astype(F32)
    base_mask = ((ri // INV_BASE) == (ci // INV_BASE)).astype(F32)
    level_masks = []
    size = INV_BASE
    while size < L:
        level_masks.append((((ri // (2 * size)) == (ci // (2 * size)))
                            & ((ri // size) != (ci // size))).astype(BF16))
        size *= 2

    def stack(t):
        return jnp.concatenate([t * m0, t * m1], axis=0)

    def stack_h(t):
        t = t.astype(BF16)
        return jnp.concatenate([t * m0h, t * m1h], axis=0)

    def chunk_tables(unit):
        c, pr = unit
        rows = slice(c * L, (c + 1) * L)
        cols = slice(pr * LANES, (pr + 1) * LANES)
        return dict(a=stack_h(at_s[rows, cols]), r=stack(rt_s[rows, cols]),
                    b=stack_h(bt_s[rows, cols]), k=stack_h(kt_s[rows, cols]),
                    v=stack_h(v_s[rows, cols]), bh=stack_h(bh_s[rows, cols]),
                    kh=stack_h(kh_s[rows, cols]), ptot=pt_s[c * L:c * L + 1, cols])

    def sub_block(r0):
        rows = slice(r0, r0 + SUB)
        p = p_ref[rows, :]
        prev_tail = tail[...] if r0 == 0 else p_ref[r0 - SUBLANES:r0, :]
        xs = p + (_shift_rows(p, prev_tail, 1) - p) * mu_ref[...]

        r = xs[:, 0:D_RWKV]
        k = xs[:, D_RWKV:2 * D_RWKV]
        v = xs[:, 2 * D_RWKV:3 * D_RWKV]
        xw = xs[:, 3 * D_RWKV:3 * D_RWKV + LANES]
        xa = xs[:, 3 * D_RWKV + LANES:3 * D_RWKV + 2 * LANES]
        xg = xs[:, 3 * D_RWKV + 2 * LANES:D_R]

        w = _log_sigmoid(w0_ref[...] + _dot(jnp.tanh(xw), ww2_ref[...])) - 0.5
        a = jax.nn.sigmoid(a0_ref[...] + _dot(xa, wa2_ref[...]))
        g = _dot(jax.nn.sigmoid(xg), wg2_ref[...])
        seg = seg_ref[...]
        kk = k * kk_ref[...]
        kk = kk * lax.rsqrt(jnp.maximum(_head_sums(kk * kk, seg), 1e-24))
        k2 = k * (1.0 + (a - 1.0) * ka_ref[...])
        bonus = _head_sums(r * k2 * rk_ref[...], seg)

        lw = -jnp.exp(w)
        cum = _dot_sel_l(tri_ref[...], lw)
        tot = jnp.concatenate([jnp.broadcast_to(cum[c * L + L - 1:c * L + L, :], (L, D_RWKV))
                               for c in range(SUB // L)], axis=0)
        e_neg = jnp.exp(-cum)
        e_end = jnp.exp(tot - cum)
        b = kk * a
        rt_s[rows, :] = r * jnp.exp(cum)
        at_s[rows, :] = -kk * jnp.exp(cum - lw)
        bt_s[rows, :] = b * e_neg
        kt_s[rows, :] = k2 * e_neg
        bh_s[rows, :] = b * e_end
        kh_s[rows, :] = k2 * e_end
        v_s[rows, :] = v
        pt_s[rows, :] = jnp.exp(tot)

        chunks = range(r0 // L, (r0 + SUB) // L)
        units = [(c, pr) for c in chunks for pr in range(RWKV_PAIRS)]
        tabs = [chunk_tables(u) for u in units]
        prods = [_dot_nt(jnp.concatenate([t["a"], t["r"].astype(BF16)], axis=0),
                         jnp.concatenate([t["b"], t["k"]], axis=0)) for t in tabs]
        n_ab = [x[0:2 * L, 0:2 * L] * tril_s for x in prods]
        a_ak = [x[0:2 * L, 2 * L:4 * L].astype(BF16) * tril_sh for x in prods]
        a_rb = [x[2 * L:4 * L, 0:2 * L].astype(BF16) * tril_ih for x in prods]
        a_rk = [x[2 * L:4 * L, 2 * L:4 * L].astype(BF16) * tril_ih for x in prods]
        n_abh = [n.astype(BF16) for n in n_ab]
        inv = [eye + n * base_mask for n in n_ab]
        for lvl_mask in level_masks:
            invh = [m.astype(BF16) for m in inv]
            cm = [_dot(n * lvl_mask, mh) for n, mh in zip(n_abh, invh)]
            inv = [m + _dot(mh, x) for m, mh, x in zip(inv, invh, cm)]
        av = [_dot(jnp.concatenate([x, y], axis=0), t["v"]) for x, y, t in zip(a_ak, a_rk, tabs)]
        wm = [_dot(m, jnp.concatenate([t["a"], x[0:2 * L].astype(BF16)], axis=1))
              for m, t, x in zip(inv, tabs, av)]
        q = [_dot(x, y) for x, y in zip(a_rb, wm)]
        tn = [_dot_tn(t["bh"], y) for t, y in zip(tabs, wm)]
        kv = [_dot_tn(t["kh"], t["v"]) for t in tabs]
        r_p = [t["r"] + x[:, 0:LANES] for t, x in zip(tabs, q)]
        y_pre = [x[:, LANES:2 * LANES] + y[2 * L:4 * L] for x, y in zip(q, av)]
        gmat = [eye * t["ptot"] + x[:, 0:LANES] for t, x in zip(tabs, tn)]
        t_pre = [x[:, LANES:2 * LANES] + y for x, y in zip(tn, kv)]

        y_chunks = []
        for j in range(len(chunks)):
            us = [j * RWKV_PAIRS + pr for pr in range(RWKV_PAIRS)]
            sts = [state[pr] for pr in range(RWKV_PAIRS)]
            both = [_dot(jnp.concatenate([r_p[u], gmat[u]], axis=0), st) for u, st in zip(us, sts)]
            y_pairs = []
            for pr, (u, z) in enumerate(zip(us, both)):
                y_st = z[0:2 * L] + y_pre[u]
                state[pr] = (z[2 * L:4 * L] + t_pre[u]) * blockdiag
                y_pairs.append(y_st[0:L, :] + y_st[L:2 * L, :])
            y_chunks.append(jnp.concatenate(y_pairs, axis=1))
        y = jnp.concatenate(y_chunks, axis=0)

        mean = _head_sums(y, seg) * (1.0 / RWKV_HEAD)
        yc = y - mean
        var = _head_sums(yc * yc, seg) * (1.0 / RWKV_HEAD)
        yn = yc * lax.rsqrt(var + LNX_EPS) * lnw_ref[...] + lnb_ref[...]
        o_ref[rows, :] = ((yn + bonus * v) * g).astype(o_ref.dtype)

    for r0 in range(0, TB, SUB):
        sub_block(r0)
    tail[...] = p_ref[TB - SUBLANES:TB, :]


def _rwkv(p_r, mu, ww2, w0, wa2, a0, wg2, k_k, k_a, r_k, lnx_w, lnx_b):
    B, T, _ = p_r.shape
    TB, L = TB_RWKV, RWKV_CHUNK
    iseg = jnp.arange(SEG_TILE)
    seg = (iseg[:, None] // RWKV_HEAD == iseg[None, :] // RWKV_HEAD).astype(BF16)
    it = jnp.arange(RWKV_SUB)
    same_chunk = it[:, None] // L == it[None, :] // L
    tri = (same_chunk & (it[:, None] >= it[None, :])).astype(BF16)
    row = lambda n: _const_spec((1, n))
    blk = lambda n: pl.BlockSpec((None, TB, n), lambda b, t: (b, t, 0))
    scr = lambda: pltpu.VMEM((TB, D_RWKV), F32)
    return pl.pallas_call(
        _rwkv_kernel,
        grid=(B, T // TB),
        in_specs=[blk(D_R), row(D_R), _const_spec((LANES, D_RWKV)), row(D_RWKV),
                  _const_spec((LANES, D_RWKV)), row(D_RWKV), _const_spec((G_LORA, D_RWKV)),
                  row(D_RWKV), row(D_RWKV), row(D_RWKV), row(D_RWKV), row(D_RWKV),
                  _const_spec((SEG_TILE, SEG_TILE)), _const_spec((RWKV_SUB, RWKV_SUB))],
        out_specs=blk(D_RWKV),
        out_shape=jax.ShapeDtypeStruct((B, T, D_RWKV), BF16),
        scratch_shapes=[pltpu.VMEM((SUBLANES, D_R), F32)] + [scr() for _ in range(8)]
                       + [pltpu.VMEM((RWKV_PAIRS, LANES, LANES), F32)],
        compiler_params=pltpu.CompilerParams(dimension_semantics=("arbitrary", "arbitrary"),
                                             vmem_limit_bytes=VMEM_LIMIT),
        name="rwkv",
    )(p_r, mu, ww2, w0, wa2, a0, wg2, k_k, k_a, r_k, lnx_w, lnx_b, seg, tri)


def _chunk_cummax(x, pos):
    d = 1
    while d < MLSTM_CHUNK:
        x = jnp.maximum(x, jnp.where(pos >= d, pltpu.roll(x, d, axis=0), -jnp.inf))
        d *= 2
    return x


def _mlstm_kernel(pm_ref, pg_ref, grow_ref, cw_ref, cb_ref, gbc_ref, gbr_ref, mhn_ref, tri_ref,
                  trit_ref, rep_ref, seg_ref, o_ref, tail, q_s, k_s, h_s, cn_state, m_state):
    TB, L, H, N = TB_MLSTM, MLSTM_CHUNK, MLSTM_HEADS, MLSTM_HEAD
    DQK = 2 * D_MLSTM
    NC = TB // L
    t_idx = pl.program_id(1)

    @pl.when(t_idx == 0)
    def _():
        tail[...] = jnp.zeros(tail.shape, F32)
        cn_state[...] = jnp.zeros(cn_state.shape, F32)
        m_state[...] = jnp.zeros(m_state.shape, F32)

    x = pm_ref[:, 0:DQK]
    prev_tail = tail[...]
    tail[...] = x[TB - SUBLANES:TB, :]
    acc = cb_ref[...] + cw_ref[QK_CONV - 1:QK_CONV, :] * x
    for j in range(QK_CONV - 1):
        acc = acc + cw_ref[j:j + 1, :] * _shift_rows(x, prev_tail, QK_CONV - 1 - j)
    qk = _silu(acc)
    q_s[...] = qk[:, 0:D_MLSTM]
    k_s[...] = qk[:, D_MLSTM:DQK] * (N ** -0.5)

    lane = lax.broadcasted_iota(jnp.int32, (1, LANES), 1)
    gc = pg_ref[...] + gbc_ref[...]
    gc = jnp.where(lane < H, gc, jnp.where(lane < 2 * H, _log_sigmoid(gc), 0.0))
    rep = _dot_sel_r(gc, rep_ref[...])
    ig_c = rep[:, 0:H * LANES]
    b_c = _dot_sel_l(tri_ref[...], rep[:, H * LANES:2 * H * LANES])
    g_c = ig_c - b_c
    gr = grow_ref[...] + gbr_ref[:, 0:1]
    sub = lax.broadcasted_iota(jnp.int32, (N_GATES, 1), 0)
    gr = jnp.where(sub < H, gr, _log_sigmoid(gr))
    g_r = gr[0:H, :] - _dot_sel_r(gr, trit_ref[...])[H:2 * H, :]

    pos = lax.broadcasted_iota(jnp.int32, (TB, LANES), 0) % L
    ti = lax.broadcasted_iota(jnp.int32, (L, L), 0)
    si = lax.broadcasted_iota(jnp.int32, (L, L), 1)
    causal = ti >= si
    ones_v = jnp.ones((L, N), F32)

    a_in, m_in = {}, {}
    for h in range(H):
        cols = slice(h * LANES, (h + 1) * LANES)
        cmax = _chunk_cummax(g_c[:, cols], pos)
        m = m_state[h]
        for c in range(NC):
            a = jnp.maximum(m, cmax[c * L:(c + 1) * L, :])
            a_in[(c, h)], m_in[(c, h)] = a, m
            m = b_c[c * L + L - 1:c * L + L, cols] + a[L - 1:L, :]
        m_state[h] = m

    units = [(c, h) for c in range(NC) for h in range(H)]

    def load(c, h):
        rows = slice(c * L, (c + 1) * L)
        cols = slice(h * N, (h + 1) * N)
        return (q_s[rows, cols], k_s[rows, cols],
                pm_ref[rows, 2 * D_MLSTM + h * N:2 * D_MLSTM + (h + 1) * N])

    qkv = {u: load(*u) for u in units}
    dw = {(c, h): jnp.exp(jnp.where(causal, g_r[h:h + 1, c * L:(c + 1) * L] - a_in[(c, h)][:, 0:L],
                                    -jnp.inf)) for c, h in units}
    iw = {u: jnp.exp(m_in[u] - a_in[u]) for u in units}
    s = {u: _dot_nt(qkv[u][0], qkv[u][1]) * dw[u] for u in units}
    sv = {u: _dot(s[u], jnp.concatenate([qkv[u][2], ones_v], axis=1)) for u in units}
    a_end = {u: a_in[u][L - 1:L, :] for u in units}
    ws = {(c, h): jnp.exp(g_c[c * L:(c + 1) * L, h * LANES:(h + 1) * LANES] - a_end[(c, h)])
          for c, h in units}
    dec = {u: jnp.exp(m_in[u] - a_end[u]) for u in units}
    kv = {u: _dot_tn(qkv[u][1], jnp.concatenate([ws[u] * qkv[u][2], ws[u]], axis=1)) for u in units}
    cn_in = {}
    for h in range(H):
        cn = cn_state[h]
        for c in range(NC):
            cn_in[(c, h)] = cn
            cn = jnp.concatenate([dec[(c, h)], dec[(c, h)]], axis=1) * cn + kv[(c, h)]
        cn_state[h] = cn
    qc = {u: _dot(qkv[u][0], cn_in[u]) for u in units}
    for c, h in units:
        u = (c, h)
        num = sv[u][:, 0:N] + iw[u] * qc[u][:, 0:N]
        den = sv[u][:, N:2 * N] + iw[u] * qc[u][:, N:2 * N]
        m_t = b_c[c * L:(c + 1) * L, h * LANES:(h + 1) * LANES] + a_in[u]
        h_s[c * L:(c + 1) * L, h * N:(h + 1) * N] = num / jnp.maximum(jnp.abs(den), jnp.exp(-m_t))

    hh = h_s[...]
    seg = seg_ref[...]
    mean = _head_sums(hh, seg) * (1.0 / N)
    yc = hh - mean
    var = _head_sums(yc * yc, seg) * (1.0 / N)
    yn = yc * lax.rsqrt(var + MHN_EPS) * mhn_ref[...]
    o_ref[...] = (yn * jax.nn.sigmoid(pm_ref[:, 3 * D_MLSTM:4 * D_MLSTM])).astype(o_ref.dtype)


def _mlstm(p_m, p_g, g_row, conv_w, conv_b, gb_col, gb_row, mhn_w):
    B, T, _ = p_m.shape
    TB, L = TB_MLSTM, MLSTM_CHUNK
    it = jnp.arange(TB)
    same_chunk = it[:, None] // L == it[None, :] // L
    tri = (same_chunk & (it[:, None] >= it[None, :])).astype(BF16)
    rep = (jnp.arange(LANES)[:, None] == jnp.arange(N_GATES * LANES)[None, :] // LANES).astype(BF16)
    iseg = jnp.arange(SEG_TILE)
    seg = (iseg[:, None] // MLSTM_HEAD == iseg[None, :] // MLSTM_HEAD).astype(BF16)
    blk = lambda n: pl.BlockSpec((None, TB, n), lambda b, t: (b, t, 0))
    return pl.pallas_call(
        _mlstm_kernel,
        grid=(B, T // TB),
        in_specs=[blk(D_M), blk(LANES),
                  pl.BlockSpec((None, N_GATES, TB), lambda b, t: (b, 0, t)),
                  _const_spec((QK_CONV, 2 * D_MLSTM)), _const_spec((1, 2 * D_MLSTM)),
                  _const_spec((1, LANES)), _const_spec((N_GATES, LANES)),
                  _const_spec((1, D_MLSTM)), _const_spec((TB, TB)), _const_spec((TB, TB)),
                  _const_spec((LANES, N_GATES * LANES)), _const_spec((SEG_TILE, SEG_TILE))],
        out_specs=blk(D_MLSTM),
        out_shape=jax.ShapeDtypeStruct((B, T, D_MLSTM), BF16),
        scratch_shapes=[pltpu.VMEM((SUBLANES, 2 * D_MLSTM), F32),
                        pltpu.VMEM((TB, D_MLSTM), F32), pltpu.VMEM((TB, D_MLSTM), F32),
                        pltpu.VMEM((TB, D_MLSTM), F32),
                        pltpu.VMEM((MLSTM_HEADS, MLSTM_HEAD, 2 * MLSTM_HEAD), F32),
                        pltpu.VMEM((MLSTM_HEADS, 1, LANES), F32)],
        compiler_params=pltpu.CompilerParams(dimension_semantics=("arbitrary", "arbitrary"),
                                             vmem_limit_bytes=VMEM_LIMIT),
        name="mlstm",
    )(p_m, p_g, g_row, conv_w, conv_b, gb_col, gb_row, mhn_w, tri, tri.T, rep, seg)


def _ffn_kernel(x_ref, yr_ref, ym_ref, mod_ref, wout_ref, n2g_ref, wup_ref, cw_ref, cb_ref,
                wdn_ref, nfg_ref, o_ref, tail, act_s):
    TM = TM_FFN
    t_idx = pl.program_id(1)

    @pl.when(t_idx == 0)
    def _():
        tail[...] = jnp.zeros(tail.shape, F32)

    y = (jnp.dot(yr_ref[...], wout_ref[0:D_RWKV, :], preferred_element_type=F32)
         + jnp.dot(ym_ref[...], wout_ref[D_RWKV:2 * D_RWKV, :], preferred_element_type=F32))
    x1 = x_ref[...] + mod_ref[2:3, :] * y
    h = _rms_mod(x1, n2g_ref[...], mod_ref[3:4, :], mod_ref[4:5, :]).astype(BF16)

    def conv(u, col0):
        cols = slice(col0, col0 + FF_TILE)
        prev_tail = tail[:, cols]
        tail[:, cols] = u[TM - SUBLANES:TM, :]
        out = cb_ref[:, cols] + cw_ref[FF_CONV - 1:FF_CONV, cols] * u
        for j in range(FF_CONV - 1):
            out = out + cw_ref[j:j + 1, cols] * _shift_rows(u, prev_tail, FF_CONV - 1 - j)
        return out

    for j in range(D_FF // FF_TILE):
        ca = j * FF_TILE
        cl = D_FF + j * FF_TILE
        ua = jnp.dot(h, wup_ref[:, ca:ca + FF_TILE], preferred_element_type=F32)
        ul = jnp.dot(h, wup_ref[:, cl:cl + FF_TILE], preferred_element_type=F32)
        act_s[:, ca:ca + FF_TILE] = (_silu(conv(ua, ca)) * conv(ul, cl)).astype(BF16)

    ffn = jnp.dot(act_s[...], wdn_ref[...], preferred_element_type=F32)
    x2 = x1 + mod_ref[5:6, :] * ffn
    o_ref[...] = (x2 * lax.rsqrt(jnp.mean(x2 * x2, axis=-1, keepdims=True) + NORM_EPS)
                  * nfg_ref[...])


def _ffn(x, y_r, y_m, mod, w_out, norm2_g, w_up, conv_w, conv_b, w_down, normf_g):
    B, T, _ = x.shape
    TM = TM_FFN
    blk = lambda n: pl.BlockSpec((None, TM, n), lambda b, t: (b, t, 0))
    return pl.pallas_call(
        _ffn_kernel,
        grid=(B, T // TM),
        in_specs=[blk(D_MODEL), blk(D_RWKV), blk(D_MLSTM),
                  pl.BlockSpec((None, 6, D_MODEL), lambda b, t: (b, 0, 0)),
                  _const_spec((D_MODEL, D_MODEL)), _const_spec((1, D_MODEL)),
                  _const_spec((D_MODEL, 2 * D_FF)), _const_spec((FF_CONV, 2 * D_FF)),
                  _const_spec((1, 2 * D_FF)), _const_spec((D_FF, D_MODEL)),
                  _const_spec((1, D_MODEL))],
        out_specs=blk(D_MODEL),
        out_shape=jax.ShapeDtypeStruct((B, T, D_MODEL), F32),
        scratch_shapes=[pltpu.VMEM((SUBLANES, 2 * D_FF), F32),
                        pltpu.VMEM((TM, D_FF), BF16)],
        compiler_params=pltpu.CompilerParams(dimension_semantics=("arbitrary", "arbitrary"),
                                             vmem_limit_bytes=VMEM_LIMIT),
        name="ffn",
    )(x, y_r, y_m, mod, w_out, norm2_g, w_up, conv_w, conv_b, w_down, normf_g)


def _pad_cols(a, n):
    return jnp.pad(a, ((0, 0), (0, n - a.shape[1])))


def _pad_rows(a, n):
    return jnp.pad(a, ((0, n - a.shape[0]), (0, 0)))


def kernel(x, c, w_ada, b_ada, norm1_g, norm2_g, normf_g, w_in, mu_rwkv, w_w2, w0, w_a2, a0, w_g2,
           k_k, k_a, r_k, lnx_w, lnx_b, conv_qk_w, conv_qk_b, i_bias, f_bias, mhn_w, w_out, w_up,
           conv_ff_w, conv_ff_b, w_down):
    assert w_ada.shape[0] == 1, "single-layer block"
    row = lambda a: a.reshape(1, -1)
    R3 = 3 * D_RWKV
    d_rin = R3 + W_LORA + A_LORA + G_LORA

    wi = w_in[0]

    def rwkv_cols(a):
        return jnp.concatenate([a[:, 0:R3], _pad_cols(a[:, R3:R3 + W_LORA], LANES),
                                _pad_cols(a[:, R3 + W_LORA:R3 + W_LORA + A_LORA], LANES),
                                a[:, R3 + W_LORA + A_LORA:d_rin]], axis=1)

    w_gate = wi[:, d_rin + D_M:]
    w_all = jnp.concatenate([rwkv_cols(wi[:, :d_rin]), wi[:, d_rin:d_rin + D_M],
                             _pad_cols(w_gate, LANES)], axis=1).astype(BF16)
    mu = rwkv_cols(row(mu_rwkv[0]))
    gates_b = jnp.concatenate([i_bias[0], f_bias[0]])
    gb_col = _pad_cols(row(gates_b), LANES)
    gb_row = jnp.broadcast_to(gates_b[:, None], (N_GATES, LANES))

    mod = _ada(c, w_ada, b_ada)
    mod = jnp.transpose(mod, (1, 0, 2))
    p_r, p_m, p_g, g_row = _inproj(x, mod, row(norm1_g[0]), w_all)
    y_r = _rwkv(p_r, mu, _pad_rows(w_w2[0], LANES).astype(BF16), row(w0[0]),
                _pad_rows(w_a2[0], LANES).astype(BF16), row(a0[0]), w_g2[0].astype(BF16),
                row(k_k[0]), row(k_a[0]), row(r_k[0]), row(lnx_w[0]), row(lnx_b[0]))
    y_m = _mlstm(p_m, p_g, g_row, conv_qk_w[0], row(conv_qk_b[0]), gb_col, gb_row, row(mhn_w[0]))
    return _ffn(x, y_r, y_m, mod, w_out[0].astype(BF16), row(norm2_g[0]), w_up[0].astype(BF16),
                conv_ff_w[0], row(conv_ff_b[0]), w_down[0].astype(BF16), row(normf_g))
```

```python
import jax
import jax.numpy as jnp
from jax import lax
from jax.experimental import pallas as pl
from jax.experimental.pallas import tpu as pltpu

D_MODEL = 1024
D_RWKV = 512
RWKV_HEAD = 64
RWKV_HEADS = 8
RWKV_PAIRS = RWKV_HEADS // 2
W_LORA = 64
A_LORA = 64
G_LORA = 128
D_MLSTM = 512
MLSTM_HEADS = 4
MLSTM_HEAD = 128
QK_CONV = 4
MLSTM_CHUNK = 64
D_FF = 2816
FF_CONV = 3
NORM_EPS = 1e-6
LNX_EPS = 64e-5
MHN_EPS = 1e-6

LANES = 128
SUBLANES = 8
RWKV_CHUNK = 64
INV_BASE = 2
SEG_TILE = 256
RWKV_SUB = 256
D_R = 3 * D_RWKV + 3 * LANES
D_M = 4 * D_MLSTM
D_P = D_R + D_M + LANES
N_GATES = 2 * MLSTM_HEADS

TM_IN = 1024
IN_SUB = 512
IN_COLS = 512
TB_RWKV = 512
TB_MLSTM = 1024
MLSTM_SUB = 256
TM_FFN = 512
FF_TILE = 256
VMEM_LIMIT = 56 * 1024 * 1024

F32 = jnp.float32
BF16 = jnp.bfloat16


def _dot(a, b):
    return jnp.dot(a.astype(BF16), b.astype(BF16), preferred_element_type=F32)


def _dot_nt(a, b):
    return lax.dot_general(a.astype(BF16), b.astype(BF16), (((1,), (1,)), ((), ())),
                           preferred_element_type=F32)


def _dot_tn(a, b):
    return lax.dot_general(a.astype(BF16), b.astype(BF16), (((0,), (0,)), ((), ())),
                           preferred_element_type=F32)


def _head_sums(x, seg):
    hi = x.astype(BF16)
    lo = (x - hi.astype(F32)).astype(BF16)
    d = lambda t: jnp.dot(t, seg, preferred_element_type=F32)
    tiles = [d(hi[:, c:c + SEG_TILE]) + d(lo[:, c:c + SEG_TILE])
             for c in range(0, x.shape[1], SEG_TILE)]
    return jnp.concatenate(tiles, axis=1)


def _split3(x):
    hi = x.astype(BF16)
    r1 = x - hi.astype(F32)
    mid = r1.astype(BF16)
    lo = (r1 - mid.astype(F32)).astype(BF16)
    return hi, mid, lo


def _dot_sel_l(sel, x):
    hi, mid, lo = _split3(x)
    d = lambda t: jnp.dot(sel, t, preferred_element_type=F32)
    return d(hi) + (d(mid) + d(lo))


def _dot_sel_r(x, sel):
    hi, mid, lo = _split3(x)
    d = lambda t: jnp.dot(t, sel, preferred_element_type=F32)
    return d(hi) + (d(mid) + d(lo))


def _log_sigmoid(z):
    return jnp.minimum(z, 0.0) - jnp.log(1.0 + jnp.exp(-jnp.abs(z)))


def _silu(z):
    return z * jax.nn.sigmoid(z)


def _shift_rows(u, tail, d):
    rolled = pltpu.roll(u, d, axis=0)
    sub = lax.broadcasted_iota(jnp.int32, (SUBLANES, u.shape[1]), 0)
    head = jnp.where(sub < d, pltpu.roll(tail, d, axis=0), rolled[0:SUBLANES])
    return jnp.concatenate([head, rolled[SUBLANES:]], axis=0)


def _drain(gen):
    for _ in gen:
        pass


def _interleave(main, fillers):
    for _ in main:
        for f in fillers:
            next(f, None)
    for f in fillers:
        _drain(f)


def _const_spec(shape):
    nd = len(shape)
    return pl.BlockSpec(shape, lambda *_: (0,) * nd, pipeline_mode=pl.Buffered(1))


def _ada_kernel(c_ref, w_ref, b_ref, o_ref):
    s = _silu(c_ref[...])
    o_ref[0] = jnp.dot(s, w_ref[...], preferred_element_type=F32,
                       precision=lax.Precision.HIGHEST) + b_ref[...]


def _ada(c, w_ada, b_ada):
    B = c.shape[0]
    return pl.pallas_call(
        _ada_kernel,
        grid=(6,),
        in_specs=[pl.BlockSpec((B, D_MODEL), lambda j: (0, 0)),
                  pl.BlockSpec((None, D_MODEL, D_MODEL), lambda j: (0, 0, j)),
                  pl.BlockSpec((1, D_MODEL), lambda j: (0, j))],
        out_specs=pl.BlockSpec((1, B, D_MODEL), lambda j: (j, 0, 0)),
        out_shape=jax.ShapeDtypeStruct((6, B, D_MODEL), F32),
        compiler_params=pltpu.CompilerParams(dimension_semantics=("arbitrary",),
                                             vmem_limit_bytes=VMEM_LIMIT),
        name="ada",
    )(c, w_ada, b_ada)


def _rms_mod(x, g, shift, scale):
    y = x * lax.rsqrt(jnp.mean(x * x, axis=-1, keepdims=True) + NORM_EPS) * g
    return y * (1.0 + scale) + shift


def _inproj_kernel(x_ref, mod_ref, g_ref, w_ref, pr_ref, pm_ref, pg_ref, grow_ref):
    normed = {}

    def prologue(r0):
        rows = slice(r0, r0 + IN_SUB)
        normed[r0] = _rms_mod(x_ref[rows, :], g_ref[...], mod_ref[0:1, :],
                              mod_ref[1:2, :]).astype(BF16)
        yield

    def project(r0):
        rows = slice(r0, r0 + IN_SUB)
        h = normed.pop(r0)
        for c0 in range(0, D_R, IN_COLS):
            c1 = min(c0 + IN_COLS, D_R)
            pr_ref[rows, c0:c1] = jnp.dot(h, w_ref[:, c0:c1], preferred_element_type=F32)
            yield
        for c0 in range(0, D_M, IN_COLS):
            pm_ref[rows, c0:c0 + IN_COLS] = jnp.dot(h, w_ref[:, D_R + c0:D_R + c0 + IN_COLS],
                                                    preferred_element_type=F32)
            yield
        pg = jnp.dot(h, w_ref[:, D_R + D_M:D_P], preferred_element_type=F32)
        pg_ref[rows, :] = pg
        grow_ref[:, rows] = pg.T[0:N_GATES, :]
        yield

    starts = list(range(0, TM_IN, IN_SUB))
    pros = {r0: prologue(r0) for r0 in starts}
    _drain(pros[starts[0]])
    for i, r0 in enumerate(starts):
        _interleave(project(r0), [pros[starts[i + 1]]] if i + 1 < len(starts) else [])


def _inproj(x, mod, norm_g, w_all):
    B, T, _ = x.shape
    return pl.pallas_call(
        _inproj_kernel,
        grid=(B, T // TM_IN),
        in_specs=[pl.BlockSpec((None, TM_IN, D_MODEL), lambda b, t: (b, t, 0)),
                  pl.BlockSpec((None, 6, D_MODEL), lambda b, t: (b, 0, 0)),
                  _const_spec((1, D_MODEL)),
                  _const_spec((D_MODEL, D_P))],
        out_specs=[pl.BlockSpec((None, TM_IN, D_R), lambda b, t: (b, t, 0)),
                   pl.BlockSpec((None, TM_IN, D_M), lambda b, t: (b, t, 0)),
                   pl.BlockSpec((None, TM_IN, LANES), lambda b, t: (b, t, 0)),
                   pl.BlockSpec((None, N_GATES, TM_IN), lambda b, t: (b, 0, t))],
        out_shape=[jax.ShapeDtypeStruct((B, T, D_R), F32),
                   jax.ShapeDtypeStruct((B, T, D_M), F32),
                   jax.ShapeDtypeStruct((B, T, LANES), F32),
                   jax.ShapeDtypeStruct((B, N_GATES, T), F32)],
        compiler_params=pltpu.CompilerParams(dimension_semantics=("arbitrary", "arbitrary"),
                                             vmem_limit_bytes=VMEM_LIMIT),
        name="inproj",
    )(x, mod, norm_g, w_all)


def _rwkv_kernel(p_ref, mu_ref, ww2_ref, w0_ref, wa2_ref, a0_ref, wg2_ref, kk_ref, ka_ref, rk_ref,
                 lnw_ref, lnb_ref, seg_ref, tri_ref, o_ref,
                 tail, rt_s, at_s, bt_s, kt_s, bh_s, kh_s, v_s, pt_s, state):
    TB, L, SUB = TB_RWKV, RWKV_CHUNK, RWKV_SUB
    t_idx = pl.program_id(1)

    @pl.when(t_idx == 0)
    def _():
        tail[...] = jnp.zeros(tail.shape, F32)
        state[...] = jnp.zeros(state.shape, F32)

    lane = lax.broadcasted_iota(jnp.int32, (1, LANES), 1)
    m0 = (lane < RWKV_HEAD).astype(F32)
    m1 = 1.0 - m0
    m0h, m1h = m0.astype(BF16), m1.astype(BF16)
    ri = lax.broadcasted_iota(jnp.int32, (2 * L, 2 * L), 0)
    ci = lax.broadcasted_iota(jnp.int32, (2 * L, 2 * L), 1)
    same = (ri // L) == (ci // L)
    tril_s = (same & (ri > ci)).astype(F32)
    tril_i = (same & (ri >= ci)).astype(F32)
    tril_sh, tril_ih = tril_s.astype(BF16), tril_i.astype(BF16)
    eye = (ri == ci).astype(F32)
    blockdiag = ((ri // RWKV_HEAD) == (ci // RWKV_HEAD)).astype(F32)
    base_mask = ((ri // INV_BASE) == (ci // INV_BASE)).astype(F32)
    level_masks = []
    size = INV_BASE
    while size < L:
        level_masks.append((((ri // (2 * size)) == (ci // (2 * size)))
                            & ((ri // size) != (ci // size))).astype(BF16))
        size *= 2

    def stack(t):
        return jnp.concatenate([t * m0, t * m1], axis=0)

    def stack_h(t):
        t = t.astype(BF16)
        return jnp.concatenate([t * m0h, t * m1h], axis=0)

    def chunk_tables(unit):
        c, pr = unit
        rows = slice(c * L, (c + 1) * L)
        cols = slice(pr * LANES, (pr + 1) * LANES)
        return dict(a=stack_h(at_s[rows, cols]), r=stack(rt_s[rows, cols]),
                    b=stack_h(bt_s[rows, cols]), k=stack_h(kt_s[rows, cols]),
                    v=stack_h(v_s[rows, cols]), bh=stack_h(bh_s[rows, cols]),
                    kh=stack_h(kh_s[rows, cols]), ptot=pt_s[c * L:c * L + 1, cols])

    handoff = {}

    def prologue(r0):
        rows = slice(r0, r0 + SUB)
        p = p_ref[rows, :]
        prev_tail = tail[...] if r0 == 0 else p_ref[r0 - SUBLANES:r0, :]
        xs = p + (_shift_rows(p, prev_tail, 1) - p) * mu_ref[...]
        r = xs[:, 0:D_RWKV]
        k = xs[:, D_RWKV:2 * D_RWKV]
        v = xs[:, 2 * D_RWKV:3 * D_RWKV]
        xw = xs[:, 3 * D_RWKV:3 * D_RWKV + LANES]
        xa = xs[:, 3 * D_RWKV + LANES:3 * D_RWKV + 2 * LANES]
        xg = xs[:, 3 * D_RWKV + 2 * LANES:D_R]
        yield
        w = _log_sigmoid(w0_ref[...] + _dot(jnp.tanh(xw), ww2_ref[...])) - 0.5
        yield
        a = jax.nn.sigmoid(a0_ref[...] + _dot(xa, wa2_ref[...]))
        g = _dot(jax.nn.sigmoid(xg), wg2_ref[...])
        yield
        seg = seg_ref[...]
        kk = k * kk_ref[...]
        kk = kk * lax.rsqrt(jnp.maximum(_head_sums(kk * kk, seg), 1e-24))
        yield
        k2 = k * (1.0 + (a - 1.0) * ka_ref[...])
        bonus = _head_sums(r * k2 * rk_ref[...], seg)
        handoff[r0] = dict(v=v, bonus=bonus, g=g)
        yield
        lw = -jnp.exp(w)
        cum = _dot_sel_l(tri_ref[...], lw)
        tot = jnp.concatenate([jnp.broadcast_to(cum[c * L + L - 1:c * L + L, :], (L, D_RWKV))
                               for c in range(SUB // L)], axis=0)
        yield
        b = kk * a
        rt_s[rows, :] = r * jnp.exp(cum)
        at_s[rows, :] = -kk * jnp.exp(cum - lw)
        yield
        e_neg = jnp.exp(-cum)
        bt_s[rows, :] = b * e_neg
        kt_s[rows, :] = k2 * e_neg
        yield
        e_end = jnp.exp(tot - cum)
        bh_s[rows, :] = b * e_end
        kh_s[rows, :] = k2 * e_end
        yield
        v_s[rows, :] = v
        pt_s[rows, :] = jnp.exp(tot)

    def algebra(r0):
        chunks = range(r0 // L, (r0 + SUB) // L)
        units = [(c, pr) for c in chunks for pr in range(RWKV_PAIRS)]
        tabs = [chunk_tables(u) for u in units]
        yield
        prods = [_dot_nt(jnp.concatenate([t["a"], t["r"].astype(BF16)], axis=0),
                         jnp.concatenate([t["b"], t["k"]], axis=0)) for t in tabs]
        yield
        n_ab = [x[0:2 * L, 0:2 * L] * tril_s for x in prods]
        a_ak = [x[0:2 * L, 2 * L:4 * L].astype(BF16) * tril_sh for x in prods]
        a_rb = [x[2 * L:4 * L, 0:2 * L].astype(BF16) * tril_ih for x in prods]
        a_rk = [x[2 * L:4 * L, 2 * L:4 * L].astype(BF16) * tril_ih for x in prods]
        n_abh = [n.astype(BF16) for n in n_ab]
        inv = [eye + n * base_mask for n in n_ab]
        for lvl_mask in level_masks:
            invh = [m.astype(BF16) for m in inv]
            cm = [_dot(n * lvl_mask, mh) for n, mh in zip(n_abh, invh)]
            yield
            inv = [m + _dot(mh, x) for m, mh, x in zip(inv, invh, cm)]
            yield
        av = [_dot(jnp.concatenate([x, y], axis=0), t["v"]) for x, y, t in zip(a_ak, a_rk, tabs)]
        yield
        wm = [_dot(m, jnp.concatenate([t["a"], x[0:2 * L].astype(BF16)], axis=1))
              for m, t, x in zip(inv, tabs, av)]
        yield
        q = [_dot(x, y) for x, y in zip(a_rb, wm)]
        yield
        tn = [_dot_tn(t["bh"], y) for t, y in zip(tabs, wm)]
        yield
        kv = [_dot_tn(t["kh"], t["v"]) for t in tabs]
        r_p = [t["r"] + x[:, 0:LANES] for t, x in zip(tabs, q)]
        y_pre = [x[:, LANES:2 * LANES] + y[2 * L:4 * L] for x, y in zip(q, av)]
        gmat = [eye * t["ptot"] + x[:, 0:LANES] for t, x in zip(tabs, tn)]
        t_pre = [x[:, LANES:2 * LANES] + y for x, y in zip(tn, kv)]
        yield
        y_chunks = []
        for j in range(len(chunks)):
            us = [j * RWKV_PAIRS + pr for pr in range(RWKV_PAIRS)]
            sts = [state[pr] for pr in range(RWKV_PAIRS)]
            both = [_dot(jnp.concatenate([r_p[u], gmat[u]], axis=0), st) for u, st in zip(us, sts)]
            y_pairs = []
            for pr, (u, z) in enumerate(zip(us, both)):
                y_st = z[0:2 * L] + y_pre[u]
                state[pr] = (z[2 * L:4 * L] + t_pre[u]) * blockdiag
                y_pairs.append(y_st[0:L, :] + y_st[L:2 * L, :])
            y_chunks.append(jnp.concatenate(y_pairs, axis=1))
            yield
        handoff[r0]["y"] = jnp.concatenate(y_chunks, axis=0)

    def epilogue(r0):
        h = handoff.pop(r0)
        seg = seg_ref[...]
        y = h["y"]
        mean = _head_sums(y, seg) * (1.0 / RWKV_HEAD)
        yield
        yc = y - mean
        var = _head_sums(yc * yc, seg) * (1.0 / RWKV_HEAD)
        yield
        yn = yc * lax.rsqrt(var + LNX_EPS) * lnw_ref[...] + lnb_ref[...]
        o_ref[r0:r0 + SUB, :] = ((yn + h["bonus"] * h["v"]) * h["g"]).astype(o_ref.dtype)

    starts = list(range(0, TB, SUB))
    pros = {r0: prologue(r0) for r0 in starts}
    _drain(pros[starts[0]])
    for i, r0 in enumerate(starts):
        fillers = []
        if i + 1 < len(starts):
            fillers.append(pros[starts[i + 1]])
        if i >= 1:
            fillers.append(epilogue(starts[i - 1]))
        _interleave(algebra(r0), fillers)
    _drain(epilogue(starts[-1]))
    tail[...] = p_ref[TB - SUBLANES:TB, :]


def _rwkv(p_r, mu, ww2, w0, wa2, a0, wg2, k_k, k_a, r_k, lnx_w, lnx_b):
    B, T, _ = p_r.shape
    TB, L = TB_RWKV, RWKV_CHUNK
    iseg = jnp.arange(SEG_TILE)
    seg = (iseg[:, None] // RWKV_HEAD == iseg[None, :] // RWKV_HEAD).astype(BF16)
    it = jnp.arange(RWKV_SUB)
    same_chunk = it[:, None] // L == it[None, :] // L
    tri = (same_chunk & (it[:, None] >= it[None, :])).astype(BF16)
    row = lambda n: _const_spec((1, n))
    blk = lambda n: pl.BlockSpec((None, TB, n), lambda b, t: (b, t, 0))
    scr = lambda: pltpu.VMEM((TB, D_RWKV), F32)
    return pl.pallas_call(
        _rwkv_kernel,
        grid=(B, T // TB),
        in_specs=[blk(D_R), row(D_R), _const_spec((LANES, D_RWKV)), row(D_RWKV),
                  _const_spec((LANES, D_RWKV)), row(D_RWKV), _const_spec((G_LORA, D_RWKV)),
                  row(D_RWKV), row(D_RWKV), row(D_RWKV), row(D_RWKV), row(D_RWKV),
                  _const_spec((SEG_TILE, SEG_TILE)), _const_spec((RWKV_SUB, RWKV_SUB))],
        out_specs=blk(D_RWKV),
        out_shape=jax.ShapeDtypeStruct((B, T, D_RWKV), BF16),
        scratch_shapes=[pltpu.VMEM((SUBLANES, D_R), F32)] + [scr() for _ in range(8)]
                       + [pltpu.VMEM((RWKV_PAIRS, LANES, LANES), F32)],
        compiler_params=pltpu.CompilerParams(dimension_semantics=("arbitrary", "arbitrary"),
                                             vmem_limit_bytes=VMEM_LIMIT),
        name="rwkv",
    )(p_r, mu, ww2, w0, wa2, a0, wg2, k_k, k_a, r_k, lnx_w, lnx_b, seg, tri)


def _chunk_cummax(x, pos):
    d = 1
    while d < MLSTM_CHUNK:
        x = jnp.maximum(x, jnp.where(pos >= d, pltpu.roll(x, d, axis=0), -jnp.inf))
        d *= 2
    return x


def _mlstm_kernel(pm_ref, pg_ref, grow_ref, cw_ref, cb_ref, gbc_ref, gbr_ref, mhn_ref, tri_ref,
                  trit_ref, rep_ref, seg_ref, o_ref, tail, q_s, k_s, cn_state, m_state):
    TB, SUB, L, H, N = TB_MLSTM, MLSTM_SUB, MLSTM_CHUNK, MLSTM_HEADS, MLSTM_HEAD
    DQK = 2 * D_MLSTM
    NC = SUB // L
    t_idx = pl.program_id(1)

    @pl.when(t_idx == 0)
    def _():
        tail[...] = jnp.zeros(tail.shape, F32)
        cn_state[...] = jnp.zeros(cn_state.shape, F32)
        m_state[...] = jnp.zeros(m_state.shape, F32)

    lane = lax.broadcasted_iota(jnp.int32, (1, LANES), 1)
    sub = lax.broadcasted_iota(jnp.int32, (N_GATES, 1), 0)
    pos = lax.broadcasted_iota(jnp.int32, (SUB, LANES), 0) % L
    ti = lax.broadcasted_iota(jnp.int32, (L, L), 0)
    si = lax.broadcasted_iota(jnp.int32, (L, L), 1)
    causal = ti >= si
    ones_v = jnp.ones((L, N), F32)
    handoff = {}

    def prologue(r0):
        rows = slice(r0, r0 + SUB)
        gc = pg_ref[rows, :] + gbc_ref[...]
        gc = jnp.where(lane < H, gc, jnp.where(lane < 2 * H, _log_sigmoid(gc), 0.0))
        rep = _dot_sel_r(gc, rep_ref[...])
        yield
        ig_c = rep[:, 0:H * LANES]
        b_c = _dot_sel_l(tri_ref[...], rep[:, H * LANES:2 * H * LANES])
        g_c = ig_c - b_c
        yield
        gr = grow_ref[:, rows] + gbr_ref[:, 0:1]
        gr = jnp.where(sub < H, gr, _log_sigmoid(gr))
        g_r = gr[0:H, :] - _dot_sel_r(gr, trit_ref[...])[H:2 * H, :]
        yield
        x = pm_ref[rows, 0:DQK]
        prev_tail = tail[...] if r0 == 0 else pm_ref[r0 - SUBLANES:r0, 0:DQK]
        acc = cb_ref[...] + cw_ref[QK_CONV - 1:QK_CONV, :] * x
        for j in range(QK_CONV - 1):
            acc = acc + cw_ref[j:j + 1, :] * _shift_rows(x, prev_tail, QK_CONV - 1 - j)
            yield
        qk = _silu(acc)
        q_s[rows, :] = qk[:, 0:D_MLSTM]
        k_s[rows, :] = qk[:, D_MLSTM:DQK] * (N ** -0.5)
        yield
        a_in, m_in = {}, {}
        for h in range(H):
            cols = slice(h * LANES, (h + 1) * LANES)
            cmax = _chunk_cummax(g_c[:, cols], pos)
            m = m_state[h]
            for c in range(NC):
                a = jnp.maximum(m, cmax[c * L:(c + 1) * L, :])
                a_in[(c, h)], m_in[(c, h)] = a, m
                m = b_c[c * L + L - 1:c * L + L, cols] + a[L - 1:L, :]
            m_state[h] = m
            yield
        handoff[r0] = dict(a_in=a_in, m_in=m_in, g_c=g_c, b_c=b_c, g_r=g_r)

    def mix(r0):
        hd = handoff[r0]
        a_in, m_in, g_c, b_c, g_r = hd["a_in"], hd["m_in"], hd["g_c"], hd["b_c"], hd["g_r"]
        units = [(c, h) for c in range(NC) for h in range(H)]

        def load(c, h):
            rows = slice(r0 + c * L, r0 + (c + 1) * L)
            cols = slice(h * N, (h + 1) * N)
            return (q_s[rows, cols], k_s[rows, cols],
                    pm_ref[rows, 2 * D_MLSTM + h * N:2 * D_MLSTM + (h + 1) * N])

        qkv = {u: load(*u) for u in units}
        dw = {(c, h): jnp.exp(jnp.where(causal, g_r[h:h + 1, c * L:(c + 1) * L]
                                        - a_in[(c, h)][:, 0:L], -jnp.inf)) for c, h in units}
        yield
        iw = {u: jnp.exp(m_in[u] - a_in[u]) for u in units}
        s = {u: _dot_nt(qkv[u][0], qkv[u][1]) * dw[u] for u in units}
        yield
        sv = {u: _dot(s[u], jnp.concatenate([qkv[u][2], ones_v], axis=1)) for u in units}
        yield
        a_end = {u: a_in[u][L - 1:L, :] for u in units}
        ws = {(c, h): jnp.exp(g_c[c * L:(c + 1) * L, h * LANES:(h + 1) * LANES] - a_end[(c, h)])
              for c, h in units}
        dec = {u: jnp.exp(m_in[u] - a_end[u]) for u in units}
        yield
        kv = {u: _dot_tn(qkv[u][1], jnp.concatenate([ws[u] * qkv[u][2], ws[u]], axis=1))
              for u in units}
        yield
        cn_in = {}
        for h in range(H):
            cn = cn_state[h]
            for c in range(NC):
                cn_in[(c, h)] = cn
                cn = jnp.concatenate([dec[(c, h)], dec[(c, h)]], axis=1) * cn + kv[(c, h)]
            cn_state[h] = cn
        yield
        qc = {u: _dot(qkv[u][0], cn_in[u]) for u in units}
        yield
        hs = {}
        for c, h in units:
            u = (c, h)
            num = sv[u][:, 0:N] + iw[u] * qc[u][:, 0:N]
            den = sv[u][:, N:2 * N] + iw[u] * qc[u][:, N:2 * N]
            m_t = b_c[c * L:(c + 1) * L, h * LANES:(h + 1) * LANES] + a_in[u]
            hs[u] = num / jnp.maximum(jnp.abs(den), jnp.exp(-m_t))
            if h == H - 1:
                yield
        hd["hh"] = jnp.concatenate([jnp.concatenate([hs[(c, h)] for h in range(H)], axis=1)
                                    for c in range(NC)], axis=0)

    def epilogue(r0):
        rows = slice(r0, r0 + SUB)
        hh = handoff.pop(r0)["hh"]
        seg = seg_ref[...]
        mean = _head_sums(hh, seg) * (1.0 / N)
        yield
        yc = hh - mean
        var = _head_sums(yc * yc, seg) * (1.0 / N)
        yield
        yn = yc * lax.rsqrt(var + MHN_EPS) * mhn_ref[...]
        o_ref[rows, :] = (yn * jax.nn.sigmoid(pm_ref[rows, 3 * D_MLSTM:4 * D_MLSTM])).astype(o_ref.dtype)

    starts = list(range(0, TB, SUB))
    pros = {r0: prologue(r0) for r0 in starts}
    _drain(pros[starts[0]])
    for i, r0 in enumerate(starts):
        fillers = []
        if i + 1 < len(starts):
            fillers.append(pros[starts[i + 1]])
        if i >= 1:
            fillers.append(epilogue(starts[i - 1]))
        _interleave(mix(r0), fillers)
    _drain(epilogue(starts[-1]))
    tail[...] = pm_ref[TB - SUBLANES:TB, 0:DQK]


def _mlstm(p_m, p_g, g_row, conv_w, conv_b, gb_col, gb_row, mhn_w):
    B, T, _ = p_m.shape
    TB, SUB, L = TB_MLSTM, MLSTM_SUB, MLSTM_CHUNK
    it = jnp.arange(SUB)
    same_chunk = it[:, None] // L == it[None, :] // L
    tri = (same_chunk & (it[:, None] >= it[None, :])).astype(BF16)
    rep = (jnp.arange(LANES)[:, None] == jnp.arange(N_GATES * LANES)[None, :] // LANES).astype(BF16)
    iseg = jnp.arange(SEG_TILE)
    seg = (iseg[:, None] // MLSTM_HEAD == iseg[None, :] // MLSTM_HEAD).astype(BF16)
    blk = lambda n: pl.BlockSpec((None, TB, n), lambda b, t: (b, t, 0))
    return pl.pallas_call(
        _mlstm_kernel,
        grid=(B, T // TB),
        in_specs=[blk(D_M), blk(LANES),
                  pl.BlockSpec((None, N_GATES, TB), lambda b, t: (b, 0, t)),
                  _const_spec((QK_CONV, 2 * D_MLSTM)), _const_spec((1, 2 * D_MLSTM)),
                  _const_spec((1, LANES)), _const_spec((N_GATES, LANES)),
                  _const_spec((1, D_MLSTM)), _const_spec((SUB, SUB)), _const_spec((SUB, SUB)),
                  _const_spec((LANES, N_GATES * LANES)), _const_spec((SEG_TILE, SEG_TILE))],
        out_specs=blk(D_MLSTM),
        out_shape=jax.ShapeDtypeStruct((B, T, D_MLSTM), BF16),
        scratch_shapes=[pltpu.VMEM((SUBLANES, 2 * D_MLSTM), F32),
                        pltpu.VMEM((TB, D_MLSTM), F32), pltpu.VMEM((TB, D_MLSTM), F32),
                        pltpu.VMEM((MLSTM_HEADS, MLSTM_HEAD, 2 * MLSTM_HEAD), F32),
                        pltpu.VMEM((MLSTM_HEADS, 1, LANES), F32)],
        compiler_params=pltpu.CompilerParams(dimension_semantics=("arbitrary", "arbitrary"),
                                             vmem_limit_bytes=VMEM_LIMIT),
        name="mlstm",
    )(p_m, p_g, g_row, conv_w, conv_b, gb_col, gb_row, mhn_w, tri, tri.T, rep, seg)


def _ffn_kernel(x_ref, yr_ref, ym_ref, mod_ref, wout_ref, n2g_ref, wup_ref, cw_ref, cb_ref,
                wdn_ref, nfg_ref, o_ref, tail, act_s):
    TM = TM_FFN
    t_idx = pl.program_id(1)

    @pl.when(t_idx == 0)
    def _():
        tail[...] = jnp.zeros(tail.shape, F32)

    y = (jnp.dot(yr_ref[...], wout_ref[0:D_RWKV, :], preferred_element_type=F32)
         + jnp.dot(ym_ref[...], wout_ref[D_RWKV:2 * D_RWKV, :], preferred_element_type=F32))
    x1 = x_ref[...] + mod_ref[2:3, :] * y
    h = _rms_mod(x1, n2g_ref[...], mod_ref[3:4, :], mod_ref[4:5, :]).astype(BF16)

    def conv(u, col0):
        cols = slice(col0, col0 + FF_TILE)
        prev_tail = tail[:, cols]
        tail[:, cols] = u[TM - SUBLANES:TM, :]
        out = cb_ref[:, cols] + cw_ref[FF_CONV - 1:FF_CONV, cols] * u
        for j in range(FF_CONV - 1):
            out = out + cw_ref[j:j + 1, cols] * _shift_rows(u, prev_tail, FF_CONV - 1 - j)
        return out

    for j in range(D_FF // FF_TILE):
        ca = j * FF_TILE
        cl = D_FF + j * FF_TILE
        ua = jnp.dot(h, wup_ref[:, ca:ca + FF_TILE], preferred_element_type=F32)
        ul = jnp.dot(h, wup_ref[:, cl:cl + FF_TILE], preferred_element_type=F32)
        act_s[:, ca:ca + FF_TILE] = (_silu(conv(ua, ca)) * conv(ul, cl)).astype(BF16)

    ffn = jnp.dot(act_s[...], wdn_ref[...], preferred_element_type=F32)
    x2 = x1 + mod_ref[5:6, :] * ffn
    o_ref[...] = (x2 * lax.rsqrt(jnp.mean(x2 * x2, axis=-1, keepdims=True) + NORM_EPS)
                  * nfg_ref[...])


def _ffn(x, y_r, y_m, mod, w_out, norm2_g, w_up, conv_w, conv_b, w_down, normf_g):
    B, T, _ = x.shape
    TM = TM_FFN
    blk = lambda n: pl.BlockSpec((None, TM, n), lambda b, t: (b, t, 0))
    return pl.pallas_call(
        _ffn_kernel,
        grid=(B, T // TM),
        in_specs=[blk(D_MODEL), blk(D_RWKV), blk(D_MLSTM),
                  pl.BlockSpec((None, 6, D_MODEL), lambda b, t: (b, 0, 0)),
                  _const_spec((D_MODEL, D_MODEL)), _const_spec((1, D_MODEL)),
                  _const_spec((D_MODEL, 2 * D_FF)), _const_spec((FF_CONV, 2 * D_FF)),
                  _const_spec((1, 2 * D_FF)), _const_spec((D_FF, D_MODEL)),
                  _const_spec((1, D_MODEL))],
        out_specs=blk(D_MODEL),
        out_shape=jax.ShapeDtypeStruct((B, T, D_MODEL), F32),
        scratch_shapes=[pltpu.VMEM((SUBLANES, 2 * D_FF), F32),
                        pltpu.VMEM((TM, D_FF), BF16)],
        compiler_params=pltpu.CompilerParams(dimension_semantics=("arbitrary", "arbitrary"),
                                             vmem_limit_bytes=VMEM_LIMIT),
        name="ffn",
    )(x, y_r, y_m, mod, w_out, norm2_g, w_up, conv_w, conv_b, w_down, normf_g)


def _pad_cols(a, n):
    return jnp.pad(a, ((0, 0), (0, n - a.shape[1])))


def _pad_rows(a, n):
    return jnp.pad(a, ((0, n - a.shape[0]), (0, 0)))


def kernel(x, c, w_ada, b_ada, norm1_g, norm2_g, normf_g, w_in, mu_rwkv, w_w2, w0, w_a2, a0, w_g2,
           k_k, k_a, r_k, lnx_w, lnx_b, conv_qk_w, conv_qk_b, i_bias, f_bias, mhn_w, w_out, w_up,
           conv_ff_w, conv_ff_b, w_down):
    assert w_ada.shape[0] == 1, "single-layer block"
    row = lambda a: a.reshape(1, -1)
    R3 = 3 * D_RWKV
    d_rin = R3 + W_LORA + A_LORA + G_LORA

    wi = w_in[0]

    def rwkv_cols(a):
        return jnp.concatenate([a[:, 0:R3], _pad_cols(a[:, R3:R3 + W_LORA], LANES),
                                _pad_cols(a[:, R3 + W_LORA:R3 + W_LORA + A_LORA], LANES),
                                a[:, R3 + W_LORA + A_LORA:d_rin]], axis=1)

    w_gate = wi[:, d_rin + D_M:]
    w_all = jnp.concatenate([rwkv_cols(wi[:, :d_rin]), wi[:, d_rin:d_rin + D_M],
                             _pad_cols(w_gate, LANES)], axis=1).astype(BF16)
    mu = rwkv_cols(row(mu_rwkv[0]))
    gates_b = jnp.concatenate([i_bias[0], f_bias[0]])
    gb_col = _pad_cols(row(gates_b), LANES)
    gb_row = jnp.broadcast_to(gates_b[:, None], (N_GATES, LANES))

    mod = _ada(c, w_ada, b_ada)
    mod = jnp.transpose(mod, (1, 0, 2))
    p_r, p_m, p_g, g_row = _inproj(x, mod, row(norm1_g[0]), w_all)
    y_r = _rwkv(p_r, mu, _pad_rows(w_w2[0], LANES).astype(BF16), row(w0[0]),
                _pad_rows(w_a2[0], LANES).astype(BF16), row(a0[0]), w_g2[0].astype(BF16),
                row(k_k[0]), row(k_a[0]), row(r_k[0]), row(lnx_w[0]), row(lnx_b[0]))
    y_m = _mlstm(p_m, p_g, g_row, conv_qk_w[0], row(conv_qk_b[0]), gb_col, gb_row, row(mhn_w[0]))
    return _ffn(x, y_r, y_m, mod, w_out[0].astype(BF16), row(norm2_g[0]), w_up[0].astype(BF16),
                conv_ff_w[0], row(conv_ff_b[0]), w_down[0].astype(BF16), row(normf_g))
```

```python
import jax
import jax.numpy as jnp
from jax import lax
from jax.experimental import pallas as pl
from jax.experimental.pallas import tpu as pltpu

D_MODEL = 1024
D_RWKV = 512
RWKV_HEAD = 64
RWKV_HEADS = 8
RWKV_PAIRS = RWKV_HEADS // 2
W_LORA = 64
A_LORA = 64
G_LORA = 128
D_MLSTM = 512
MLSTM_HEADS = 4
MLSTM_HEAD = 128
QK_CONV = 4
MLSTM_CHUNK = 64
D_FF = 2816
FF_CONV = 3
NORM_EPS = 1e-6
LNX_EPS = 64e-5
MHN_EPS = 1e-6

LANES = 128
SUBLANES = 8
RWKV_CHUNK = 64
INV_BASE = 2
SEG_TILE = 256
RWKV_SUB = 256
D_R = 3 * D_RWKV + 3 * LANES
D_M = 4 * D_MLSTM
D_P = D_R + D_M + LANES
N_GATES = 2 * MLSTM_HEADS

TM_IN = 1024
IN_SUB = 512
IN_COLS = 512
TB_RWKV = 1024
TB_MLSTM = 1024
MLSTM_SUB = 256
TM_FFN = 512
FF_TILE = 256
VMEM_LIMIT = 56 * 1024 * 1024

F32 = jnp.float32
BF16 = jnp.bfloat16


def _dot(a, b):
    return jnp.dot(a.astype(BF16), b.astype(BF16), preferred_element_type=F32)


def _dot_nt(a, b):
    return lax.dot_general(a.astype(BF16), b.astype(BF16), (((1,), (1,)), ((), ())),
                           preferred_element_type=F32)


def _dot_tn(a, b):
    return lax.dot_general(a.astype(BF16), b.astype(BF16), (((0,), (0,)), ((), ())),
                           preferred_element_type=F32)


def _head_sums(x, seg):
    hi = x.astype(BF16)
    lo = (x - hi.astype(F32)).astype(BF16)
    d = lambda t: jnp.dot(t, seg, preferred_element_type=F32)
    tiles = [d(hi[:, c:c + SEG_TILE]) + d(lo[:, c:c + SEG_TILE])
             for c in range(0, x.shape[1], SEG_TILE)]
    return jnp.concatenate(tiles, axis=1)


def _split3(x):
    hi = x.astype(BF16)
    r1 = x - hi.astype(F32)
    mid = r1.astype(BF16)
    lo = (r1 - mid.astype(F32)).astype(BF16)
    return hi, mid, lo


def _dot_sel_l(sel, x):
    hi, mid, lo = _split3(x)
    d = lambda t: jnp.dot(sel, t, preferred_element_type=F32)
    return d(hi) + (d(mid) + d(lo))


def _dot_sel_r(x, sel):
    hi, mid, lo = _split3(x)
    d = lambda t: jnp.dot(t, sel, preferred_element_type=F32)
    return d(hi) + (d(mid) + d(lo))


def _log_sigmoid(z):
    return jnp.minimum(z, 0.0) - jnp.log(1.0 + jnp.exp(-jnp.abs(z)))


def _silu(z):
    return z * jax.nn.sigmoid(z)


def _shift_rows(u, tail, d):
    rolled = pltpu.roll(u, d, axis=0)
    sub = lax.broadcasted_iota(jnp.int32, (SUBLANES, u.shape[1]), 0)
    head = jnp.where(sub < d, pltpu.roll(tail, d, axis=0), rolled[0:SUBLANES])
    return jnp.concatenate([head, rolled[SUBLANES:]], axis=0)


def _drain(gen):
    for _ in gen:
        pass


def _interleave(main, fillers):
    for _ in main:
        for f in fillers:
            next(f, None)
    for f in fillers:
        _drain(f)


def _const_spec(shape):
    nd = len(shape)
    return pl.BlockSpec(shape, lambda *_: (0,) * nd, pipeline_mode=pl.Buffered(1))


def _ada_kernel(c_ref, w_ref, b_ref, o_ref):
    s = _silu(c_ref[...])
    o_ref[0] = jnp.dot(s, w_ref[...], preferred_element_type=F32,
                       precision=lax.Precision.HIGHEST) + b_ref[...]


def _ada(c, w_ada, b_ada):
    B = c.shape[0]
    return pl.pallas_call(
        _ada_kernel,
        grid=(6,),
        in_specs=[pl.BlockSpec((B, D_MODEL), lambda j: (0, 0)),
                  pl.BlockSpec((None, D_MODEL, D_MODEL), lambda j: (0, 0, j)),
                  pl.BlockSpec((1, D_MODEL), lambda j: (0, j))],
        out_specs=pl.BlockSpec((1, B, D_MODEL), lambda j: (j, 0, 0)),
        out_shape=jax.ShapeDtypeStruct((6, B, D_MODEL), F32),
        compiler_params=pltpu.CompilerParams(dimension_semantics=("arbitrary",),
                                             vmem_limit_bytes=VMEM_LIMIT),
        name="ada",
    )(c, w_ada, b_ada)


def _rms_mod(x, g, shift, scale):
    y = x * lax.rsqrt(jnp.mean(x * x, axis=-1, keepdims=True) + NORM_EPS) * g
    return y * (1.0 + scale) + shift


def _inproj_kernel(x_ref, mod_ref, g_ref, w_ref, pr_ref, pm_ref, pg_ref, grow_ref):
    normed = {}

    def prologue(r0):
        rows = slice(r0, r0 + IN_SUB)
        normed[r0] = _rms_mod(x_ref[rows, :], g_ref[...], mod_ref[0:1, :],
                              mod_ref[1:2, :]).astype(BF16)
        yield

    def project(r0):
        rows = slice(r0, r0 + IN_SUB)
        h = normed.pop(r0)
        for c0 in range(0, D_R + LANES, IN_COLS):
            res = jnp.dot(h, w_ref[:, c0:c0 + IN_COLS], preferred_element_type=F32)
            if c0 + IN_COLS <= D_R:
                pr_ref[rows, c0:c0 + IN_COLS] = res
            else:
                pr_ref[rows, c0:D_R] = res[:, 0:D_R - c0]
                pg = res[:, D_R - c0:IN_COLS]
                pg_ref[rows, :] = pg
                grow_ref[:, rows] = pg.T[0:N_GATES, :]
            yield
        for c0 in range(0, D_M, IN_COLS):
            w_c0 = D_R + LANES + c0
            pm_ref[rows, c0:c0 + IN_COLS] = jnp.dot(h, w_ref[:, w_c0:w_c0 + IN_COLS],
                                                    preferred_element_type=F32)
            yield

    starts = list(range(0, TM_IN, IN_SUB))
    pros = {r0: prologue(r0) for r0 in starts}
    _drain(pros[starts[0]])
    for i, r0 in enumerate(starts):
        _interleave(project(r0), [pros[starts[i + 1]]] if i + 1 < len(starts) else [])


def _inproj(x, mod, norm_g, w_all):
    B, T, _ = x.shape
    return pl.pallas_call(
        _inproj_kernel,
        grid=(B, T // TM_IN),
        in_specs=[pl.BlockSpec((None, TM_IN, D_MODEL), lambda b, t: (b, t, 0)),
                  pl.BlockSpec((None, 6, D_MODEL), lambda b, t: (b, 0, 0)),
                  _const_spec((1, D_MODEL)),
                  _const_spec((D_MODEL, D_P))],
        out_specs=[pl.BlockSpec((None, TM_IN, D_R), lambda b, t: (b, t, 0)),
                   pl.BlockSpec((None, TM_IN, D_M), lambda b, t: (b, t, 0)),
                   pl.BlockSpec((None, TM_IN, LANES), lambda b, t: (b, t, 0)),
                   pl.BlockSpec((None, N_GATES, TM_IN), lambda b, t: (b, 0, t))],
        out_shape=[jax.ShapeDtypeStruct((B, T, D_R), F32),
                   jax.ShapeDtypeStruct((B, T, D_M), F32),
                   jax.ShapeDtypeStruct((B, T, LANES), F32),
                   jax.ShapeDtypeStruct((B, N_GATES, T), F32)],
        compiler_params=pltpu.CompilerParams(dimension_semantics=("arbitrary", "arbitrary"),
                                             vmem_limit_bytes=VMEM_LIMIT),
        name="inproj",
    )(x, mod, norm_g, w_all)


def _rwkv_kernel(p_ref, mu_ref, ww2_ref, w0_ref, wa2_ref, a0_ref, wg2_ref, kk_ref, ka_ref, rk_ref,
                 lnw_ref, lnb_ref, seg_ref, tri_ref, o_ref,
                 tail, rt_s, at_s, bt_s, kt_s, bh_s, kh_s, v_s, pt_s, state):
    TB, L, SUB = TB_RWKV, RWKV_CHUNK, RWKV_SUB
    t_idx = pl.program_id(1)

    @pl.when(t_idx == 0)
    def _():
        tail[...] = jnp.zeros(tail.shape, F32)
        state[...] = jnp.zeros(state.shape, F32)

    lane = lax.broadcasted_iota(jnp.int32, (1, LANES), 1)
    m0 = (lane < RWKV_HEAD).astype(F32)
    m1 = 1.0 - m0
    m0h, m1h = m0.astype(BF16), m1.astype(BF16)
    ri = lax.broadcasted_iota(jnp.int32, (2 * L, 2 * L), 0)
    ci = lax.broadcasted_iota(jnp.int32, (2 * L, 2 * L), 1)
    same = (ri // L) == (ci // L)
    tril_s = (same & (ri > ci)).astype(F32)
    tril_i = (same & (ri >= ci)).astype(F32)
    tril_sh, tril_ih = tril_s.astype(BF16), tril_i.astype(BF16)
    eye = (ri == ci).astype(F32)
    blockdiag = ((ri // RWKV_HEAD) == (ci // RWKV_HEAD)).astype(F32)
    base_mask = ((ri // INV_BASE) == (ci // INV_BASE)).astype(F32)
    level_masks = []
    size = INV_BASE
    while size < L:
        level_masks.append((((ri // (2 * size)) == (ci // (2 * size)))
                            & ((ri // size) != (ci // size))).astype(BF16))
        size *= 2

    def stack(t):
        return jnp.concatenate([t * m0, t * m1], axis=0)

    def stack_h(t):
        t = t.astype(BF16)
        return jnp.concatenate([t * m0h, t * m1h], axis=0)

    def chunk_tables(unit):
        c, pr = unit
        rows = slice(c * L, (c + 1) * L)
        cols = slice(pr * LANES, (pr + 1) * LANES)
        return dict(a=stack_h(at_s[rows, cols]), r=stack(rt_s[rows, cols]),
                    b=stack_h(bt_s[rows, cols]), k=stack_h(kt_s[rows, cols]),
                    v=stack_h(v_s[rows, cols]), bh=stack_h(bh_s[rows, cols]),
                    kh=stack_h(kh_s[rows, cols]), ptot=pt_s[c * L:c * L + 1, cols])

    handoff = {}

    def prologue(r0):
        rows = slice(r0, r0 + SUB)
        p = p_ref[rows, :]
        prev_tail = tail[...] if r0 == 0 else p_ref[r0 - SUBLANES:r0, :]
        xs = p + (_shift_rows(p, prev_tail, 1) - p) * mu_ref[...]
        r = xs[:, 0:D_RWKV]
        k = xs[:, D_RWKV:2 * D_RWKV]
        v = xs[:, 2 * D_RWKV:3 * D_RWKV]
        xw = xs[:, 3 * D_RWKV:3 * D_RWKV + LANES]
        xa = xs[:, 3 * D_RWKV + LANES:3 * D_RWKV + 2 * LANES]
        xg = xs[:, 3 * D_RWKV + 2 * LANES:D_R]
        yield
        w = _log_sigmoid(w0_ref[...] + _dot(jnp.tanh(xw), ww2_ref[...])) - 0.5
        yield
        a = jax.nn.sigmoid(a0_ref[...] + _dot(xa, wa2_ref[...]))
        g = _dot(jax.nn.sigmoid(xg), wg2_ref[...])
        yield
        seg = seg_ref[...]
        kk = k * kk_ref[...]
        kk = kk * lax.rsqrt(jnp.maximum(_head_sums(kk * kk, seg), 1e-24))
        yield
        k2 = k * (1.0 + (a - 1.0) * ka_ref[...])
        bonus = _head_sums(r * k2 * rk_ref[...], seg)
        handoff[r0] = dict(v=v, bonus=bonus, g=g)
        yield
        lw = -jnp.exp(w)
        cum = _dot_sel_l(tri_ref[...], lw)
        tot = jnp.concatenate([jnp.broadcast_to(cum[c * L + L - 1:c * L + L, :], (L, D_RWKV))
                               for c in range(SUB // L)], axis=0)
        yield
        b = kk * a
        rt_s[rows, :] = r * jnp.exp(cum)
        at_s[rows, :] = -kk * jnp.exp(cum - lw)
        yield
        e_neg = jnp.exp(-cum)
        bt_s[rows, :] = b * e_neg
        kt_s[rows, :] = k2 * e_neg
        yield
        e_end = jnp.exp(tot - cum)
        bh_s[rows, :] = b * e_end
        kh_s[rows, :] = k2 * e_end
        yield
        v_s[rows, :] = v
        pt_s[rows, :] = jnp.exp(tot)

    def algebra(r0):
        chunks = range(r0 // L, (r0 + SUB) // L)
        units = [(c, pr) for c in chunks for pr in range(RWKV_PAIRS)]
        tabs = [chunk_tables(u) for u in units]
        yield
        prods = [_dot_nt(jnp.concatenate([t["a"], t["r"].astype(BF16)], axis=0),
                         jnp.concatenate([t["b"], t["k"]], axis=0)) for t in tabs]
        yield
        n_ab = [x[0:2 * L, 0:2 * L] * tril_s for x in prods]
        a_ak = [x[0:2 * L, 2 * L:4 * L].astype(BF16) * tril_sh for x in prods]
        a_rb = [x[2 * L:4 * L, 0:2 * L].astype(BF16) * tril_ih for x in prods]
        a_rk = [x[2 * L:4 * L, 2 * L:4 * L].astype(BF16) * tril_ih for x in prods]
        n_abh = [n.astype(BF16) for n in n_ab]
        inv = [eye + n * base_mask for n in n_ab]
        for lvl_mask in level_masks:
            invh = [m.astype(BF16) for m in inv]
            cm = [_dot(n * lvl_mask, mh) for n, mh in zip(n_abh, invh)]
            yield
            inv = [m + _dot(mh, x) for m, mh, x in zip(inv, invh, cm)]
            yield
        av = [_dot(jnp.concatenate([x, y], axis=0), t["v"]) for x, y, t in zip(a_ak, a_rk, tabs)]
        yield
        wm = [_dot(m, jnp.concatenate([t["a"], x[0:2 * L].astype(BF16)], axis=1))
              for m, t, x in zip(inv, tabs, av)]
        yield
        q = [_dot(x, y) for x, y in zip(a_rb, wm)]
        yield
        tn = [_dot_tn(t["bh"], y) for t, y in zip(tabs, wm)]
        yield
        kv = [_dot_tn(t["kh"], t["v"]) for t in tabs]
        r_p = [t["r"] + x[:, 0:LANES] for t, x in zip(tabs, q)]
        y_pre = [x[:, LANES:2 * LANES] + y[2 * L:4 * L] for x, y in zip(q, av)]
        gmat = [eye * t["ptot"] + x[:, 0:LANES] for t, x in zip(tabs, tn)]
        t_pre = [x[:, LANES:2 * LANES] + y for x, y in zip(tn, kv)]
        yield
        y_chunks = []
        for j in range(len(chunks)):
            us = [j * RWKV_PAIRS + pr for pr in range(RWKV_PAIRS)]
            sts = [state[pr] for pr in range(RWKV_PAIRS)]
            both = [_dot(jnp.concatenate([r_p[u], gmat[u]], axis=0), st) for u, st in zip(us, sts)]
            y_pairs = []
            for pr, (u, z) in enumerate(zip(us, both)):
                y_st = z[0:2 * L] + y_pre[u]
                state[pr] = (z[2 * L:4 * L] + t_pre[u]) * blockdiag
                y_pairs.append(y_st[0:L, :] + y_st[L:2 * L, :])
            y_chunks.append(jnp.concatenate(y_pairs, axis=1))
            yield
        handoff[r0]["y"] = jnp.concatenate(y_chunks, axis=0)

    def epilogue(r0):
        h = handoff.pop(r0)
        seg = seg_ref[...]
        y = h["y"]
        mean = _head_sums(y, seg) * (1.0 / RWKV_HEAD)
        yield
        yc = y - mean
        var = _head_sums(yc * yc, seg) * (1.0 / RWKV_HEAD)
        yield
        yn = yc * lax.rsqrt(var + LNX_EPS) * lnw_ref[...] + lnb_ref[...]
        o_ref[r0:r0 + SUB, :] = ((yn + h["bonus"] * h["v"]) * h["g"]).astype(o_ref.dtype)

    starts = list(range(0, TB, SUB))
    pros = {r0: prologue(r0) for r0 in starts}
    _drain(pros[starts[0]])
    for i, r0 in enumerate(starts):
        fillers = []
        if i + 1 < len(starts):
            fillers.append(pros[starts[i + 1]])
        if i >= 1:
            fillers.append(epilogue(starts[i - 1]))
        _interleave(algebra(r0), fillers)
    _drain(epilogue(starts[-1]))
    tail[...] = p_ref[TB - SUBLANES:TB, :]


def _rwkv(p_r, mu, ww2, w0, wa2, a0, wg2, k_k, k_a, r_k, lnx_w, lnx_b):
    B, T, _ = p_r.shape
    TB, L = TB_RWKV, RWKV_CHUNK
    iseg = jnp.arange(SEG_TILE)
    seg = (iseg[:, None] // RWKV_HEAD == iseg[None, :] // RWKV_HEAD).astype(BF16)
    it = jnp.arange(RWKV_SUB)
    same_chunk = it[:, None] // L == it[None, :] // L
    tri = (same_chunk & (it[:, None] >= it[None, :])).astype(BF16)
    row = lambda n: _const_spec((1, n))
    blk = lambda n: pl.BlockSpec((None, TB, n), lambda b, t: (b, t, 0))
    scr = lambda: pltpu.VMEM((TB, D_RWKV), F32)
    return pl.pallas_call(
        _rwkv_kernel,
        grid=(B, T // TB),
        in_specs=[blk(D_R), row(D_R), _const_spec((LANES, D_RWKV)), row(D_RWKV),
                  _const_spec((LANES, D_RWKV)), row(D_RWKV), _const_spec((G_LORA, D_RWKV)),
                  row(D_RWKV), row(D_RWKV), row(D_RWKV), row(D_RWKV), row(D_RWKV),
                  _const_spec((SEG_TILE, SEG_TILE)), _const_spec((RWKV_SUB, RWKV_SUB))],
        out_specs=blk(D_RWKV),
        out_shape=jax.ShapeDtypeStruct((B, T, D_RWKV), BF16),
        scratch_shapes=[pltpu.VMEM((SUBLANES, D_R), F32)] + [scr() for _ in range(8)]
                       + [pltpu.VMEM((RWKV_PAIRS, LANES, LANES), F32)],
        compiler_params=pltpu.CompilerParams(dimension_semantics=("arbitrary", "arbitrary"),
                                             vmem_limit_bytes=VMEM_LIMIT),
        name="rwkv",
    )(p_r, mu, ww2, w0, wa2, a0, wg2, k_k, k_a, r_k, lnx_w, lnx_b, seg, tri)


def _chunk_cummax(x, pos):
    d = 1
    while d < MLSTM_CHUNK:
        x = jnp.maximum(x, jnp.where(pos >= d, pltpu.roll(x, d, axis=0), -jnp.inf))
        d *= 2
    return x


def _mlstm_kernel(pm_ref, pg_ref, grow_ref, cw_ref, cb_ref, gbc_ref, gbr_ref, mhn_ref, tri_ref,
                  trit_ref, rep_ref, seg_ref, o_ref, tail, q_s, k_s, cn_state, m_state):
    TB, SUB, L, H, N = TB_MLSTM, MLSTM_SUB, MLSTM_CHUNK, MLSTM_HEADS, MLSTM_HEAD
    DQK = 2 * D_MLSTM
    NC = SUB // L
    t_idx = pl.program_id(1)

    @pl.when(t_idx == 0)
    def _():
        tail[...] = jnp.zeros(tail.shape, F32)
        cn_state[...] = jnp.zeros(cn_state.shape, F32)
        m_state[...] = jnp.zeros(m_state.shape, F32)

    lane = lax.broadcasted_iota(jnp.int32, (1, LANES), 1)
    sub = lax.broadcasted_iota(jnp.int32, (N_GATES, 1), 0)
    pos = lax.broadcasted_iota(jnp.int32, (SUB, LANES), 0) % L
    ti = lax.broadcasted_iota(jnp.int32, (L, L), 0)
    si = lax.broadcasted_iota(jnp.int32, (L, L), 1)
    causal = ti >= si
    ones_v = jnp.ones((L, N), F32)
    handoff = {}

    def prologue(r0):
        rows = slice(r0, r0 + SUB)
        gc = pg_ref[rows, :] + gbc_ref[...]
        gc = jnp.where(lane < H, gc, jnp.where(lane < 2 * H, _log_sigmoid(gc), 0.0))
        rep = _dot_sel_r(gc, rep_ref[...])
        yield
        ig_c = rep[:, 0:H * LANES]
        b_c = _dot_sel_l(tri_ref[...], rep[:, H * LANES:2 * H * LANES])
        g_c = ig_c - b_c
        yield
        gr = grow_ref[:, rows] + gbr_ref[:, 0:1]
        gr = jnp.where(sub < H, gr, _log_sigmoid(gr))
        g_r = gr[0:H, :] - _dot_sel_r(gr, trit_ref[...])[H:2 * H, :]
        yield
        x = pm_ref[rows, 0:DQK]
        prev_tail = tail[...] if r0 == 0 else pm_ref[r0 - SUBLANES:r0, 0:DQK]
        acc = cb_ref[...] + cw_ref[QK_CONV - 1:QK_CONV, :] * x
        for j in range(QK_CONV - 1):
            acc = acc + cw_ref[j:j + 1, :] * _shift_rows(x, prev_tail, QK_CONV - 1 - j)
            yield
        qk = _silu(acc)
        q_s[rows, :] = qk[:, 0:D_MLSTM]
        k_s[rows, :] = qk[:, D_MLSTM:DQK] * (N ** -0.5)
        yield
        a_in, m_in = {}, {}
        for h in range(H):
            cols = slice(h * LANES, (h + 1) * LANES)
            cmax = _chunk_cummax(g_c[:, cols], pos)
            m = m_state[h]
            for c in range(NC):
                a = jnp.maximum(m, cmax[c * L:(c + 1) * L, :])
                a_in[(c, h)], m_in[(c, h)] = a, m
                m = b_c[c * L + L - 1:c * L + L, cols] + a[L - 1:L, :]
            m_state[h] = m
            yield
        handoff[r0] = dict(a_in=a_in, m_in=m_in, g_c=g_c, b_c=b_c, g_r=g_r)

    def mix(r0):
        hd = handoff[r0]
        a_in, m_in, g_c, b_c, g_r = hd["a_in"], hd["m_in"], hd["g_c"], hd["b_c"], hd["g_r"]
        units = [(c, h) for c in range(NC) for h in range(H)]

        def load(c, h):
            rows = slice(r0 + c * L, r0 + (c + 1) * L)
            cols = slice(h * N, (h + 1) * N)
            return (q_s[rows, cols], k_s[rows, cols],
                    pm_ref[rows, 2 * D_MLSTM + h * N:2 * D_MLSTM + (h + 1) * N])

        qkv = {u: load(*u) for u in units}
        dw = {(c, h): jnp.exp(jnp.where(causal, g_r[h:h + 1, c * L:(c + 1) * L]
                                        - a_in[(c, h)][:, 0:L], -jnp.inf)) for c, h in units}
        yield
        iw = {u: jnp.exp(m_in[u] - a_in[u]) for u in units}
        s = {u: _dot_nt(qkv[u][0], qkv[u][1]) * dw[u] for u in units}
        yield
        sv = {u: _dot(s[u], jnp.concatenate([qkv[u][2], ones_v], axis=1)) for u in units}
        yield
        a_end = {u: a_in[u][L - 1:L, :] for u in units}
        ws = {(c, h): jnp.exp(g_c[c * L:(c + 1) * L, h * LANES:(h + 1) * LANES] - a_end[(c, h)])
              for c, h in units}
        dec = {u: jnp.exp(m_in[u] - a_end[u]) for u in units}
        yield
        kv = {u: _dot_tn(qkv[u][1], jnp.concatenate([ws[u] * qkv[u][2], ws[u]], axis=1))
              for u in units}
        yield
        cn_in = {}
        for h in range(H):
            cn = cn_state[h]
            for c in range(NC):
                cn_in[(c, h)] = cn
                cn = jnp.concatenate([dec[(c, h)], dec[(c, h)]], axis=1) * cn + kv[(c, h)]
            cn_state[h] = cn
        yield
        qc = {u: _dot(qkv[u][0], cn_in[u]) for u in units}
        yield
        hs = {}
        for c, h in units:
            u = (c, h)
            num = sv[u][:, 0:N] + iw[u] * qc[u][:, 0:N]
            den = sv[u][:, N:2 * N] + iw[u] * qc[u][:, N:2 * N]
            m_t = b_c[c * L:(c + 1) * L, h * LANES:(h + 1) * LANES] + a_in[u]
            hs[u] = num / jnp.maximum(jnp.abs(den), jnp.exp(-m_t))
            if h == H - 1:
                yield
        hd["hh"] = jnp.concatenate([jnp.concatenate([hs[(c, h)] for h in range(H)], axis=1)
                                    for c in range(NC)], axis=0)

    def epilogue(r0):
        rows = slice(r0, r0 + SUB)
        hh = handoff.pop(r0)["hh"]
        seg = seg_ref[...]
        mean = _head_sums(hh, seg) * (1.0 / N)
        yield
        yc = hh - mean
        var = _head_sums(yc * yc, seg) * (1.0 / N)
        yield
        yn = yc * lax.rsqrt(var + MHN_EPS) * mhn_ref[...]
        o_ref[rows, :] = (yn * jax.nn.sigmoid(pm_ref[rows, 3 * D_MLSTM:4 * D_MLSTM])).astype(o_ref.dtype)

    starts = list(range(0, TB, SUB))
    pros = {r0: prologue(r0) for r0 in starts}
    _drain(pros[starts[0]])
    for i, r0 in enumerate(starts):
        fillers = []
        if i + 1 < len(starts):
            fillers.append(pros[starts[i + 1]])
        if i >= 1:
            fillers.append(epilogue(starts[i - 1]))
        _interleave(mix(r0), fillers)
    _drain(epilogue(starts[-1]))
    tail[...] = pm_ref[TB - SUBLANES:TB, 0:DQK]


def _mlstm(p_m, p_g, g_row, conv_w, conv_b, gb_col, gb_row, mhn_w):
    B, T, _ = p_m.shape
    TB, SUB, L = TB_MLSTM, MLSTM_SUB, MLSTM_CHUNK
    it = jnp.arange(SUB)
    same_chunk = it[:, None] // L == it[None, :] // L
    tri = (same_chunk & (it[:, None] >= it[None, :])).astype(BF16)
    rep = (jnp.arange(LANES)[:, None] == jnp.arange(N_GATES * LANES)[None, :] // LANES).astype(BF16)
    iseg = jnp.arange(SEG_TILE)
    seg = (iseg[:, None] // MLSTM_HEAD == iseg[None, :] // MLSTM_HEAD).astype(BF16)
    blk = lambda n: pl.BlockSpec((None, TB, n), lambda b, t: (b, t, 0))
    return pl.pallas_call(
        _mlstm_kernel,
        grid=(B, T // TB),
        in_specs=[blk(D_M), blk(LANES),
                  pl.BlockSpec((None, N_GATES, TB), lambda b, t: (b, 0, t)),
                  _const_spec((QK_CONV, 2 * D_MLSTM)), _const_spec((1, 2 * D_MLSTM)),
                  _const_spec((1, LANES)), _const_spec((N_GATES, LANES)),
                  _const_spec((1, D_MLSTM)), _const_spec((SUB, SUB)), _const_spec((SUB, SUB)),
                  _const_spec((LANES, N_GATES * LANES)), _const_spec((SEG_TILE, SEG_TILE))],
        out_specs=blk(D_MLSTM),
        out_shape=jax.ShapeDtypeStruct((B, T, D_MLSTM), BF16),
        scratch_shapes=[pltpu.VMEM((SUBLANES, 2 * D_MLSTM), F32),
                        pltpu.VMEM((TB, D_MLSTM), F32), pltpu.VMEM((TB, D_MLSTM), F32),
                        pltpu.VMEM((MLSTM_HEADS, MLSTM_HEAD, 2 * MLSTM_HEAD), F32),
                        pltpu.VMEM((MLSTM_HEADS, 1, LANES), F32)],
        compiler_params=pltpu.CompilerParams(dimension_semantics=("arbitrary", "arbitrary"),
                                             vmem_limit_bytes=VMEM_LIMIT),
        name="mlstm",
    )(p_m, p_g, g_row, conv_w, conv_b, gb_col, gb_row, mhn_w, tri, tri.T, rep, seg)


def _ffn_kernel(x_ref, yr_ref, ym_ref, mod_ref, wout_ref, n2g_ref, wup_ref, cw_ref, cb_ref,
                wdn_ref, nfg_ref, o_ref, tail, act_s):
    TM = TM_FFN
    t_idx = pl.program_id(1)

    @pl.when(t_idx == 0)
    def _():
        tail[...] = jnp.zeros(tail.shape, F32)

    y = (jnp.dot(yr_ref[...], wout_ref[0:D_RWKV, :], preferred_element_type=F32)
         + jnp.dot(ym_ref[...], wout_ref[D_RWKV:2 * D_RWKV, :], preferred_element_type=F32))
    x1 = x_ref[...] + mod_ref[2:3, :] * y
    h = _rms_mod(x1, n2g_ref[...], mod_ref[3:4, :], mod_ref[4:5, :]).astype(BF16)

    def conv(u, col0):
        cols = slice(col0, col0 + FF_TILE)
        prev_tail = tail[:, cols]
        tail[:, cols] = u[TM - SUBLANES:TM, :]
        out = cb_ref[:, cols] + cw_ref[FF_CONV - 1:FF_CONV, cols] * u
        for j in range(FF_CONV - 1):
            out = out + cw_ref[j:j + 1, cols] * _shift_rows(u, prev_tail, FF_CONV - 1 - j)
        return out

    for j in range(D_FF // FF_TILE):
        ca = j * FF_TILE
        cl = D_FF + j * FF_TILE
        ua = jnp.dot(h, wup_ref[:, ca:ca + FF_TILE], preferred_element_type=F32)
        ul = jnp.dot(h, wup_ref[:, cl:cl + FF_TILE], preferred_element_type=F32)
        act_s[:, ca:ca + FF_TILE] = (_silu(conv(ua, ca)) * conv(ul, cl)).astype(BF16)

    ffn = jnp.dot(act_s[...], wdn_ref[...], preferred_element_type=F32)
    x2 = x1 + mod_ref[5:6, :] * ffn
    o_ref[...] = (x2 * lax.rsqrt(jnp.mean(x2 * x2, axis=-1, keepdims=True) + NORM_EPS)
                  * nfg_ref[...])


def _ffn(x, y_r, y_m, mod, w_out, norm2_g, w_up, conv_w, conv_b, w_down, normf_g):
    B, T, _ = x.shape
    TM = TM_FFN
    blk = lambda n: pl.BlockSpec((None, TM, n), lambda b, t: (b, t, 0))
    return pl.pallas_call(
        _ffn_kernel,
        grid=(B, T // TM),
        in_specs=[blk(D_MODEL), blk(D_RWKV), blk(D_MLSTM),
                  pl.BlockSpec((None, 6, D_MODEL), lambda b, t: (b, 0, 0)),
                  _const_spec((D_MODEL, D_MODEL)), _const_spec((1, D_MODEL)),
                  _const_spec((D_MODEL, 2 * D_FF)), _const_spec((FF_CONV, 2 * D_FF)),
                  _const_spec((1, 2 * D_FF)), _const_spec((D_FF, D_MODEL)),
                  _const_spec((1, D_MODEL))],
        out_specs=blk(D_MODEL),
        out_shape=jax.ShapeDtypeStruct((B, T, D_MODEL), F32),
        scratch_shapes=[pltpu.VMEM((SUBLANES, 2 * D_FF), F32),
                        pltpu.VMEM((TM, D_FF), BF16)],
        compiler_params=pltpu.CompilerParams(dimension_semantics=("arbitrary", "arbitrary"),
                                             vmem_limit_bytes=VMEM_LIMIT),
        name="ffn",
    )(x, y_r, y_m, mod, w_out, norm2_g, w_up, conv_w, conv_b, w_down, normf_g)


def _pad_cols(a, n):
    return jnp.pad(a, ((0, 0), (0, n - a.shape[1])))


def _pad_rows(a, n):
    return jnp.pad(a, ((0, n - a.shape[0]), (0, 0)))


def kernel(x, c, w_ada, b_ada, norm1_g, norm2_g, normf_g, w_in, mu_rwkv, w_w2, w0, w_a2, a0, w_g2,
           k_k, k_a, r_k, lnx_w, lnx_b, conv_qk_w, conv_qk_b, i_bias, f_bias, mhn_w, w_out, w_up,
           conv_ff_w, conv_ff_b, w_down):
    assert w_ada.shape[0] == 1, "single-layer block"
    row = lambda a: a.reshape(1, -1)
    R3 = 3 * D_RWKV
    d_rin = R3 + W_LORA + A_LORA + G_LORA

    wi = w_in[0]

    def rwkv_cols(a):
        return jnp.concatenate([a[:, 0:R3], _pad_cols(a[:, R3:R3 + W_LORA], LANES),
                                _pad_cols(a[:, R3 + W_LORA:R3 + W_LORA + A_LORA], LANES),
                                a[:, R3 + W_LORA + A_LORA:d_rin]], axis=1)

    w_gate = wi[:, d_rin + D_M:]
    w_all = jnp.concatenate([rwkv_cols(wi[:, :d_rin]), _pad_cols(w_gate, LANES),
                             wi[:, d_rin:d_rin + D_M]], axis=1).astype(BF16)
    mu = rwkv_cols(row(mu_rwkv[0]))
    gates_b = jnp.concatenate([i_bias[0], f_bias[0]])
    gb_col = _pad_cols(row(gates_b), LANES)
    gb_row = jnp.broadcast_to(gates_b[:, None], (N_GATES, LANES))

    mod = _ada(c, w_ada, b_ada)
    mod = jnp.transpose(mod, (1, 0, 2))
    p_r, p_m, p_g, g_row = _inproj(x, mod, row(norm1_g[0]), w_all)
    y_r = _rwkv(p_r, mu, _pad_rows(w_w2[0], LANES).astype(BF16), row(w0[0]),
                _pad_rows(w_a2[0], LANES).astype(BF16), row(a0[0]), w_g2[0].astype(BF16),
                row(k_k[0]), row(k_a[0]), row(r_k[0]), row(lnx_w[0]), row(lnx_b[0]))
    y_m = _mlstm(p_m, p_g, g_row, conv_qk_w[0], row(conv_qk_b[0]), gb_col, gb_row, row(mhn_w[0]))
    return _ffn(x, y_r, y_m, mod, w_out[0].astype(BF16), row(norm2_g[0]), w_up[0].astype(BF16),
                conv_ff_w[0], row(conv_ff_b[0]), w_down[0].astype(BF16), row(normf_g))
```

```python
import jax
import jax.numpy as jnp
from jax import lax
from jax.experimental import pallas as pl
from jax.experimental.pallas import tpu as pltpu

D_MODEL = 1024
D_RWKV = 512
RWKV_HEAD = 64
RWKV_HEADS = 8
RWKV_PAIRS = RWKV_HEADS // 2
W_LORA = 64
A_LORA = 64
G_LORA = 128
D_MLSTM = 512
MLSTM_HEADS = 4
MLSTM_HEAD = 128
QK_CONV = 4
MLSTM_CHUNK = 64
D_FF = 2816
FF_CONV = 3
NORM_EPS = 1e-6
LNX_EPS = 64e-5
MHN_EPS = 1e-6

LANES = 128
SUBLANES = 8
RWKV_CHUNK = 64
INV_BASE = 2
SEG_TILE = 256
RWKV_SUB = 256
D_R = 3 * D_RWKV + 3 * LANES
D_M = 4 * D_MLSTM
D_P = D_R + D_M + LANES
N_GATES = 2 * MLSTM_HEADS

TM_IN = 1024
IN_SUB = 512
IN_COLS = 512
TB_RWKV = 512
TB_MLSTM = 1024
MLSTM_SUB = 256
TM_FFN = 512
FF_TILE = 256
VMEM_LIMIT = 56 * 1024 * 1024

F32 = jnp.float32
BF16 = jnp.bfloat16


def _dot(a, b):
    return jnp.dot(a.astype(BF16), b.astype(BF16), preferred_element_type=F32)


def _dot_nt(a, b):
    return lax.dot_general(a.astype(BF16), b.astype(BF16), (((1,), (1,)), ((), ())),
                           preferred_element_type=F32)


def _dot_tn(a, b):
    return lax.dot_general(a.astype(BF16), b.astype(BF16), (((0,), (0,)), ((), ())),
                           preferred_element_type=F32)


def _head_sums(x, seg):
    hi = x.astype(BF16)
    lo = (x - hi.astype(F32)).astype(BF16)
    d = lambda t: jnp.dot(t, seg, preferred_element_type=F32)
    tiles = [d(hi[:, c:c + SEG_TILE]) + d(lo[:, c:c + SEG_TILE])
             for c in range(0, x.shape[1], SEG_TILE)]
    return jnp.concatenate(tiles, axis=1)


def _split3(x):
    hi = x.astype(BF16)
    r1 = x - hi.astype(F32)
    mid = r1.astype(BF16)
    lo = (r1 - mid.astype(F32)).astype(BF16)
    return hi, mid, lo


def _dot_sel_l(sel, x):
    hi, mid, lo = _split3(x)
    d = lambda t: jnp.dot(sel, t, preferred_element_type=F32)
    return d(hi) + (d(mid) + d(lo))


def _dot_sel_r(x, sel):
    hi, mid, lo = _split3(x)
    d = lambda t: jnp.dot(t, sel, preferred_element_type=F32)
    return d(hi) + (d(mid) + d(lo))


def _log_sigmoid(z):
    return jnp.minimum(z, 0.0) - jnp.log(1.0 + jnp.exp(-jnp.abs(z)))


def _silu(z):
    return z * jax.nn.sigmoid(z)


def _shift_rows(u, tail, d):
    rolled = pltpu.roll(u, d, axis=0)
    sub = lax.broadcasted_iota(jnp.int32, (SUBLANES, u.shape[1]), 0)
    head = jnp.where(sub < d, pltpu.roll(tail, d, axis=0), rolled[0:SUBLANES])
    return jnp.concatenate([head, rolled[SUBLANES:]], axis=0)


def _drain(gen):
    for _ in gen:
        pass


def _interleave(main, fillers):
    for _ in main:
        for f in fillers:
            next(f, None)
    for f in fillers:
        _drain(f)


def _const_spec(shape):
    nd = len(shape)
    return pl.BlockSpec(shape, lambda *_: (0,) * nd, pipeline_mode=pl.Buffered(1))


def _ada_kernel(c_ref, w_ref, b_ref, o_ref):
    s = _silu(c_ref[...])
    o_ref[0] = jnp.dot(s, w_ref[...], preferred_element_type=F32,
                       precision=lax.Precision.HIGHEST) + b_ref[...]


def _ada(c, w_ada, b_ada):
    B = c.shape[0]
    return pl.pallas_call(
        _ada_kernel,
        grid=(6,),
        in_specs=[pl.BlockSpec((B, D_MODEL), lambda j: (0, 0)),
                  pl.BlockSpec((None, D_MODEL, D_MODEL), lambda j: (0, 0, j)),
                  pl.BlockSpec((1, D_MODEL), lambda j: (0, j))],
        out_specs=pl.BlockSpec((1, B, D_MODEL), lambda j: (j, 0, 0)),
        out_shape=jax.ShapeDtypeStruct((6, B, D_MODEL), F32),
        compiler_params=pltpu.CompilerParams(dimension_semantics=("arbitrary",),
                                             vmem_limit_bytes=VMEM_LIMIT),
        name="ada",
    )(c, w_ada, b_ada)


def _rms_mod(x, g, shift, scale):
    y = x * lax.rsqrt(jnp.mean(x * x, axis=-1, keepdims=True) + NORM_EPS) * g
    return y * (1.0 + scale) + shift


def _inproj_kernel(x_ref, mod_ref, g_ref, w_ref, pr_ref, pm_ref, pg_ref, grow_ref):
    normed = {}

    def prologue(r0):
        rows = slice(r0, r0 + IN_SUB)
        normed[r0] = _rms_mod(x_ref[rows, :], g_ref[...], mod_ref[0:1, :],
                              mod_ref[1:2, :]).astype(BF16)
        yield

    def project(r0):
        rows = slice(r0, r0 + IN_SUB)
        h = normed.pop(r0)
        for c0 in range(0, D_R + LANES, IN_COLS):
            res = jnp.dot(h, w_ref[:, c0:c0 + IN_COLS], preferred_element_type=F32)
            if c0 + IN_COLS <= D_R:
                pr_ref[rows, c0:c0 + IN_COLS] = res
            else:
                pr_ref[rows, c0:D_R] = res[:, 0:D_R - c0]
                pg = res[:, D_R - c0:IN_COLS]
                pg_ref[rows, :] = pg
                grow_ref[:, rows] = pg.T[0:N_GATES, :]
            yield
        for c0 in range(0, D_M, IN_COLS):
            w_c0 = D_R + LANES + c0
            pm_ref[rows, c0:c0 + IN_COLS] = jnp.dot(h, w_ref[:, w_c0:w_c0 + IN_COLS],
                                                    preferred_element_type=F32)
            yield

    starts = list(range(0, TM_IN, IN_SUB))
    pros = {r0: prologue(r0) for r0 in starts}
    _drain(pros[starts[0]])
    for i, r0 in enumerate(starts):
        _interleave(project(r0), [pros[starts[i + 1]]] if i + 1 < len(starts) else [])


def _inproj(x, mod, norm_g, w_all):
    B, T, _ = x.shape
    return pl.pallas_call(
        _inproj_kernel,
        grid=(B, T // TM_IN),
        in_specs=[pl.BlockSpec((None, TM_IN, D_MODEL), lambda b, t: (b, t, 0)),
                  pl.BlockSpec((None, 6, D_MODEL), lambda b, t: (b, 0, 0)),
                  _const_spec((1, D_MODEL)),
                  _const_spec((D_MODEL, D_P))],
        out_specs=[pl.BlockSpec((None, TM_IN, D_R), lambda b, t: (b, t, 0)),
                   pl.BlockSpec((None, TM_IN, D_M), lambda b, t: (b, t, 0)),
                   pl.BlockSpec((None, TM_IN, LANES), lambda b, t: (b, t, 0)),
                   pl.BlockSpec((None, N_GATES, TM_IN), lambda b, t: (b, 0, t))],
        out_shape=[jax.ShapeDtypeStruct((B, T, D_R), F32),
                   jax.ShapeDtypeStruct((B, T, D_M), F32),
                   jax.ShapeDtypeStruct((B, T, LANES), F32),
                   jax.ShapeDtypeStruct((B, N_GATES, T), F32)],
        compiler_params=pltpu.CompilerParams(dimension_semantics=("arbitrary", "arbitrary"),
                                             vmem_limit_bytes=VMEM_LIMIT),
        name="inproj",
    )(x, mod, norm_g, w_all)


def _rwkv_kernel(p_ref, mu_ref, ww2_ref, w0_ref, wa2_ref, a0_ref, wg2_ref, kk_ref, ka_ref, rk_ref,
                 lnw_ref, lnb_ref, seg_ref, tri_ref, o_ref,
                 tail, rt_s, at_s, bt_s, kt_s, bh_s, kh_s, v_s, pt_s, state):
    TB, L, SUB = TB_RWKV, RWKV_CHUNK, RWKV_SUB
    t_idx = pl.program_id(1)

    @pl.when(t_idx == 0)
    def _():
        tail[...] = jnp.zeros(tail.shape, F32)
        state[...] = jnp.zeros(state.shape, F32)

    lane = lax.broadcasted_iota(jnp.int32, (1, LANES), 1)
    m0 = (lane < RWKV_HEAD).astype(F32)
    m1 = 1.0 - m0
    m0h, m1h = m0.astype(BF16), m1.astype(BF16)
    ri = lax.broadcasted_iota(jnp.int32, (2 * L, 2 * L), 0)
    ci = lax.broadcasted_iota(jnp.int32, (2 * L, 2 * L), 1)
    same = (ri // L) == (ci // L)
    tril_s = (same & (ri > ci)).astype(F32)
    tril_i = (same & (ri >= ci)).astype(F32)
    tril_sh, tril_ih = tril_s.astype(BF16), tril_i.astype(BF16)
    eye = (ri == ci).astype(F32)
    blockdiag = ((ri // RWKV_HEAD) == (ci // RWKV_HEAD)).astype(F32)
    base_mask = ((ri // INV_BASE) == (ci // INV_BASE)).astype(F32)
    level_masks = []
    size = INV_BASE
    while size < L:
        level_masks.append((((ri // (2 * size)) == (ci // (2 * size)))
                            & ((ri // size) != (ci // size))).astype(BF16))
        size *= 2

    def stack(t):
        return jnp.concatenate([t * m0, t * m1], axis=0)

    def stack_h(t):
        t = t.astype(BF16)
        return jnp.concatenate([t * m0h, t * m1h], axis=0)

    def chunk_tables(unit):
        c, pr = unit
        rows = slice(c * L, (c + 1) * L)
        cols = slice(pr * LANES, (pr + 1) * LANES)
        return dict(a=stack_h(at_s[rows, cols]), r=stack(rt_s[rows, cols]),
                    b=stack_h(bt_s[rows, cols]), k=stack_h(kt_s[rows, cols]),
                    v=stack_h(v_s[rows, cols]), bh=stack_h(bh_s[rows, cols]),
                    kh=stack_h(kh_s[rows, cols]), ptot=pt_s[c * L:c * L + 1, cols])

    handoff = {}

    def prologue(r0):
        rows = slice(r0, r0 + SUB)
        p = p_ref[rows, :]
        prev_tail = tail[...] if r0 == 0 else p_ref[r0 - SUBLANES:r0, :]
        xs = p + (_shift_rows(p, prev_tail, 1) - p) * mu_ref[...]
        r = xs[:, 0:D_RWKV]
        k = xs[:, D_RWKV:2 * D_RWKV]
        v = xs[:, 2 * D_RWKV:3 * D_RWKV]
        xw = xs[:, 3 * D_RWKV:3 * D_RWKV + LANES]
        xa = xs[:, 3 * D_RWKV + LANES:3 * D_RWKV + 2 * LANES]
        xg = xs[:, 3 * D_RWKV + 2 * LANES:D_R]
        yield
        w = _log_sigmoid(w0_ref[...] + _dot(jnp.tanh(xw), ww2_ref[...])) - 0.5
        yield
        a = jax.nn.sigmoid(a0_ref[...] + _dot(xa, wa2_ref[...]))
        g = _dot(jax.nn.sigmoid(xg), wg2_ref[...])
        yield
        seg = seg_ref[...]
        kk = k * kk_ref[...]
        kk = kk * lax.rsqrt(jnp.maximum(_head_sums(kk * kk, seg), 1e-24))
        yield
        k2 = k * (1.0 + (a - 1.0) * ka_ref[...])
        bonus = _head_sums(r * k2 * rk_ref[...], seg)
        handoff[r0] = dict(v=v, bonus=bonus, g=g)
        yield
        lw = -jnp.exp(w)
        cum = _dot_sel_l(tri_ref[...], lw)
        tot = jnp.concatenate([jnp.broadcast_to(cum[c * L + L - 1:c * L + L, :], (L, D_RWKV))
                               for c in range(SUB // L)], axis=0)
        yield
        b = kk * a
        rt_s[rows, :] = r * jnp.exp(cum)
        at_s[rows, :] = -kk * jnp.exp(cum - lw)
        yield
        e_neg = jnp.exp(-cum)
        bt_s[rows, :] = b * e_neg
        kt_s[rows, :] = k2 * e_neg
        yield
        e_end = jnp.exp(tot - cum)
        bh_s[rows, :] = b * e_end
        kh_s[rows, :] = k2 * e_end
        yield
        v_s[rows, :] = v
        pt_s[rows, :] = jnp.exp(tot)

    def algebra(r0):
        chunks = range(r0 // L, (r0 + SUB) // L)
        units = [(c, pr) for c in chunks for pr in range(RWKV_PAIRS)]
        tabs = [chunk_tables(u) for u in units]
        yield
        prods = [_dot_nt(jnp.concatenate([t["a"], t["r"].astype(BF16)], axis=0),
                         jnp.concatenate([t["b"], t["k"]], axis=0)) for t in tabs]
        yield
        n_ab = [x[0:2 * L, 0:2 * L] * tril_s for x in prods]
        a_ak = [x[0:2 * L, 2 * L:4 * L].astype(BF16) * tril_sh for x in prods]
        a_rb = [x[2 * L:4 * L, 0:2 * L].astype(BF16) * tril_ih for x in prods]
        a_rk = [x[2 * L:4 * L, 2 * L:4 * L].astype(BF16) * tril_ih for x in prods]
        n_abh = [n.astype(BF16) for n in n_ab]
        inv = [eye + n * base_mask for n in n_ab]
        for lvl_mask in level_masks:
            invh = [m.astype(BF16) for m in inv]
            cm = [_dot(n * lvl_mask, mh) for n, mh in zip(n_abh, invh)]
            yield
            inv = [m + _dot(mh, x) for m, mh, x in zip(inv, invh, cm)]
            yield
        av = [_dot(jnp.concatenate([x, y], axis=0), t["v"]) for x, y, t in zip(a_ak, a_rk, tabs)]
        yield
        wm = [_dot(m, jnp.concatenate([t["a"], x[0:2 * L].astype(BF16)], axis=1))
              for m, t, x in zip(inv, tabs, av)]
        yield
        q = [_dot(x, y) for x, y in zip(a_rb, wm)]
        yield
        tn = [_dot_tn(t["bh"], y) for t, y in zip(tabs, wm)]
        yield
        kv = [_dot_tn(t["kh"], t["v"]) for t in tabs]
        r_p = [t["r"] + x[:, 0:LANES] for t, x in zip(tabs, q)]
        y_pre = [x[:, LANES:2 * LANES] + y[2 * L:4 * L] for x, y in zip(q, av)]
        gmat = [eye * t["ptot"] + x[:, 0:LANES] for t, x in zip(tabs, tn)]
        t_pre = [x[:, LANES:2 * LANES] + y for x, y in zip(tn, kv)]
        yield
        y_chunks = []
        for j in range(len(chunks)):
            us = [j * RWKV_PAIRS + pr for pr in range(RWKV_PAIRS)]
            sts = [state[pr] for pr in range(RWKV_PAIRS)]
            both = [_dot(jnp.concatenate([r_p[u], gmat[u]], axis=0), st) for u, st in zip(us, sts)]
            y_pairs = []
            for pr, (u, z) in enumerate(zip(us, both)):
                y_st = z[0:2 * L] + y_pre[u]
                state[pr] = (z[2 * L:4 * L] + t_pre[u]) * blockdiag
                y_pairs.append(y_st[0:L, :] + y_st[L:2 * L, :])
            y_chunks.append(jnp.concatenate(y_pairs, axis=1))
            yield
        handoff[r0]["y"] = jnp.concatenate(y_chunks, axis=0)

    def epilogue(r0):
        h = handoff.pop(r0)
        seg = seg_ref[...]
        y = h["y"]
        mean = _head_sums(y, seg) * (1.0 / RWKV_HEAD)
        yield
        yc = y - mean
        var = _head_sums(yc * yc, seg) * (1.0 / RWKV_HEAD)
        yield
        yn = yc * lax.rsqrt(var + LNX_EPS) * lnw_ref[...] + lnb_ref[...]
        o_ref[r0:r0 + SUB, :] = ((yn + h["bonus"] * h["v"]) * h["g"]).astype(o_ref.dtype)

    starts = list(range(0, TB, SUB))
    pros = {r0: prologue(r0) for r0 in starts}
    _drain(pros[starts[0]])
    for i, r0 in enumerate(starts):
        fillers = []
        if i + 1 < len(starts):
            fillers.append(pros[starts[i + 1]])
        if i >= 1:
            fillers.append(epilogue(starts[i - 1]))
        _interleave(algebra(r0), fillers)
    _drain(epilogue(starts[-1]))
    tail[...] = p_ref[TB - SUBLANES:TB, :]


def _rwkv(p_r, mu, ww2, w0, wa2, a0, wg2, k_k, k_a, r_k, lnx_w, lnx_b):
    B, T, _ = p_r.shape
    TB, L = TB_RWKV, RWKV_CHUNK
    iseg = jnp.arange(SEG_TILE)
    seg = (iseg[:, None] // RWKV_HEAD == iseg[None, :] // RWKV_HEAD).astype(BF16)
    it = jnp.arange(RWKV_SUB)
    same_chunk = it[:, None] // L == it[None, :] // L
    tri = (same_chunk & (it[:, None] >= it[None, :])).astype(BF16)
    row = lambda n: _const_spec((1, n))
    blk = lambda n: pl.BlockSpec((None, TB, n), lambda b, t: (b, t, 0))
    scr = lambda: pltpu.VMEM((TB, D_RWKV), F32)
    return pl.pallas_call(
        _rwkv_kernel,
        grid=(B, T // TB),
        in_specs=[blk(D_R), row(D_R), _const_spec((LANES, D_RWKV)), row(D_RWKV),
                  _const_spec((LANES, D_RWKV)), row(D_RWKV), _const_spec((G_LORA, D_RWKV)),
                  row(D_RWKV), row(D_RWKV), row(D_RWKV), row(D_RWKV), row(D_RWKV),
                  _const_spec((SEG_TILE, SEG_TILE)), _const_spec((RWKV_SUB, RWKV_SUB))],
        out_specs=blk(D_RWKV),
        out_shape=jax.ShapeDtypeStruct((B, T, D_RWKV), BF16),
        scratch_shapes=[pltpu.VMEM((SUBLANES, D_R), F32)] + [scr() for _ in range(8)]
                       + [pltpu.VMEM((RWKV_PAIRS, LANES, LANES), F32)],
        compiler_params=pltpu.CompilerParams(dimension_semantics=("arbitrary", "arbitrary"),
                                             vmem_limit_bytes=VMEM_LIMIT),
        name="rwkv",
    )(p_r, mu, ww2, w0, wa2, a0, wg2, k_k, k_a, r_k, lnx_w, lnx_b, seg, tri)


def _chunk_cummax(x, pos):
    d = 1
    while d < MLSTM_CHUNK:
        x = jnp.maximum(x, jnp.where(pos >= d, pltpu.roll(x, d, axis=0), -jnp.inf))
        d *= 2
    return x


def _mlstm_kernel(pm_ref, pg_ref, grow_ref, cw_ref, cb_ref, gbc_ref, gbr_ref, mhn_ref, tri_ref,
                  trit_ref, rep_ref, seg_ref, o_ref, tail, q_s, k_s, cn_state, m_state):
    TB, SUB, L, H, N = TB_MLSTM, MLSTM_SUB, MLSTM_CHUNK, MLSTM_HEADS, MLSTM_HEAD
    DQK = 2 * D_MLSTM
    NC = SUB // L
    t_idx = pl.program_id(1)

    @pl.when(t_idx == 0)
    def _():
        tail[...] = jnp.zeros(tail.shape, F32)
        cn_state[...] = jnp.zeros(cn_state.shape, F32)
        m_state[...] = jnp.zeros(m_state.shape, F32)

    lane = lax.broadcasted_iota(jnp.int32, (1, LANES), 1)
    sub = lax.broadcasted_iota(jnp.int32, (N_GATES, 1), 0)
    pos = lax.broadcasted_iota(jnp.int32, (SUB, LANES), 0) % L
    ti = lax.broadcasted_iota(jnp.int32, (L, L), 0)
    si = lax.broadcasted_iota(jnp.int32, (L, L), 1)
    causal = ti >= si
    ones_v = jnp.ones((L, N), F32)
    handoff = {}

    def prologue(r0):
        rows = slice(r0, r0 + SUB)
        gc = pg_ref[rows, :] + gbc_ref[...]
        gc = jnp.where(lane < H, gc, jnp.where(lane < 2 * H, _log_sigmoid(gc), 0.0))
        rep = _dot_sel_r(gc, rep_ref[...])
        yield
        ig_c = rep[:, 0:H * LANES]
        b_c = _dot_sel_l(tri_ref[...], rep[:, H * LANES:2 * H * LANES])
        g_c = ig_c - b_c
        yield
        gr = grow_ref[:, rows] + gbr_ref[:, 0:1]
        gr = jnp.where(sub < H, gr, _log_sigmoid(gr))
        g_r = gr[0:H, :] - _dot_sel_r(gr, trit_ref[...])[H:2 * H, :]
        yield
        x = pm_ref[rows, 0:DQK]
        prev_tail = tail[...] if r0 == 0 else pm_ref[r0 - SUBLANES:r0, 0:DQK]
        acc = cb_ref[...] + cw_ref[QK_CONV - 1:QK_CONV, :] * x
        for j in range(QK_CONV - 1):
            acc = acc + cw_ref[j:j + 1, :] * _shift_rows(x, prev_tail, QK_CONV - 1 - j)
            yield
        qk = _silu(acc)
        q_s[rows, :] = qk[:, 0:D_MLSTM]
        k_s[rows, :] = qk[:, D_MLSTM:DQK] * (N ** -0.5)
        yield
        a_in, m_in = {}, {}
        for h in range(H):
            cols = slice(h * LANES, (h + 1) * LANES)
            cmax = _chunk_cummax(g_c[:, cols], pos)
            m = m_state[h]
            for c in range(NC):
                a = jnp.maximum(m, cmax[c * L:(c + 1) * L, :])
                a_in[(c, h)], m_in[(c, h)] = a, m
                m = b_c[c * L + L - 1:c * L + L, cols] + a[L - 1:L, :]
            m_state[h] = m
            yield
        handoff[r0] = dict(a_in=a_in, m_in=m_in, g_c=g_c, b_c=b_c, g_r=g_r)

    def mix(r0):
        hd = handoff[r0]
        a_in, m_in, g_c, b_c, g_r = hd["a_in"], hd["m_in"], hd["g_c"], hd["b_c"], hd["g_r"]
        units = [(c, h) for c in range(NC) for h in range(H)]

        def load(c, h):
            rows = slice(r0 + c * L, r0 + (c + 1) * L)
            cols = slice(h * N, (h + 1) * N)
            return (q_s[rows, cols], k_s[rows, cols],
                    pm_ref[rows, 2 * D_MLSTM + h * N:2 * D_MLSTM + (h + 1) * N])

        qkv = {u: load(*u) for u in units}
        dw = {(c, h): jnp.exp(jnp.where(causal, g_r[h:h + 1, c * L:(c + 1) * L]
                                        - a_in[(c, h)][:, 0:L], -jnp.inf)) for c, h in units}
        yield
        iw = {u: jnp.exp(m_in[u] - a_in[u]) for u in units}
        s = {u: _dot_nt(qkv[u][0], qkv[u][1]) * dw[u] for u in units}
        yield
        sv = {u: _dot(s[u], jnp.concatenate([qkv[u][2], ones_v], axis=1)) for u in units}
        yield
        a_end = {u: a_in[u][L - 1:L, :] for u in units}
        ws = {(c, h): jnp.exp(g_c[c * L:(c + 1) * L, h * LANES:(h + 1) * LANES] - a_end[(c, h)])
              for c, h in units}
        dec = {u: jnp.exp(m_in[u] - a_end[u]) for u in units}
        yield
        kv = {u: _dot_tn(qkv[u][1], jnp.concatenate([ws[u] * qkv[u][2], ws[u]], axis=1))
              for u in units}
        yield
        cn_in = {}
        for h in range(H):
            cn = cn_state[h]
            for c in range(NC):
                cn_in[(c, h)] = cn
                cn = jnp.concatenate([dec[(c, h)], dec[(c, h)]], axis=1) * cn + kv[(c, h)]
            cn_state[h] = cn
        yield
        qc = {u: _dot(qkv[u][0], cn_in[u]) for u in units}
        yield
        hs = {}
        for c, h in units:
            u = (c, h)
            num = sv[u][:, 0:N] + iw[u] * qc[u][:, 0:N]
            den = sv[u][:, N:2 * N] + iw[u] * qc[u][:, N:2 * N]
            m_t = b_c[c * L:(c + 1) * L, h * LANES:(h + 1) * LANES] + a_in[u]
            hs[u] = num / jnp.maximum(jnp.abs(den), jnp.exp(-m_t))
            if h == H - 1:
                yield
        hd["hh"] = jnp.concatenate([jnp.concatenate([hs[(c, h)] for h in range(H)], axis=1)
                                    for c in range(NC)], axis=0)

    def epilogue(r0):
        rows = slice(r0, r0 + SUB)
        hh = handoff.pop(r0)["hh"]
        seg = seg_ref[...]
        mean = _head_sums(hh, seg) * (1.0 / N)
        yield
        yc = hh - mean
        var = _head_sums(yc * yc, seg) * (1.0 / N)
        yield
        yn = yc * lax.rsqrt(var + MHN_EPS) * mhn_ref[...]
        o_ref[rows, :] = (yn * jax.nn.sigmoid(pm_ref[rows, 3 * D_MLSTM:4 * D_MLSTM])).astype(o_ref.dtype)

    starts = list(range(0, TB, SUB))
    pros = {r0: prologue(r0) for r0 in starts}
    _drain(pros[starts[0]])
    for i, r0 in enumerate(starts):
        fillers = []
        if i + 1 < len(starts):
            fillers.append(pros[starts[i + 1]])
        if i >= 1:
            fillers.append(epilogue(starts[i - 1]))
        _interleave(mix(r0), fillers)
    _drain(epilogue(starts[-1]))
    tail[...] = pm_ref[TB - SUBLANES:TB, 0:DQK]


def _mlstm(p_m, p_g, g_row, conv_w, conv_b, gb_col, gb_row, mhn_w):
    B, T, _ = p_m.shape
    TB, SUB, L = TB_MLSTM, MLSTM_SUB, MLSTM_CHUNK
    it = jnp.arange(SUB)
    same_chunk = it[:, None] // L == it[None, :] // L
    tri = (same_chunk & (it[:, None] >= it[None, :])).astype(BF16)
    rep = (jnp.arange(LANES)[:, None] == jnp.arange(N_GATES * LANES)[None, :] // LANES).astype(BF16)
    iseg = jnp.arange(SEG_TILE)
    seg = (iseg[:, None] // MLSTM_HEAD == iseg[None, :] // MLSTM_HEAD).astype(BF16)
    blk = lambda n: pl.BlockSpec((None, TB, n), lambda b, t: (b, t, 0))
    return pl.pallas_call(
        _mlstm_kernel,
        grid=(B, T // TB),
        in_specs=[blk(D_M), blk(LANES),
                  pl.BlockSpec((None, N_GATES, TB), lambda b, t: (b, 0, t)),
                  _const_spec((QK_CONV, 2 * D_MLSTM)), _const_spec((1, 2 * D_MLSTM)),
                  _const_spec((1, LANES)), _const_spec((N_GATES, LANES)),
                  _const_spec((1, D_MLSTM)), _const_spec((SUB, SUB)), _const_spec((SUB, SUB)),
                  _const_spec((LANES, N_GATES * LANES)), _const_spec((SEG_TILE, SEG_TILE))],
        out_specs=blk(D_MLSTM),
        out_shape=jax.ShapeDtypeStruct((B, T, D_MLSTM), BF16),
        scratch_shapes=[pltpu.VMEM((SUBLANES, 2 * D_MLSTM), F32),
                        pltpu.VMEM((TB, D_MLSTM), F32), pltpu.VMEM((TB, D_MLSTM), F32),
                        pltpu.VMEM((MLSTM_HEADS, MLSTM_HEAD, 2 * MLSTM_HEAD), F32),
                        pltpu.VMEM((MLSTM_HEADS, 1, LANES), F32)],
        compiler_params=pltpu.CompilerParams(dimension_semantics=("arbitrary", "arbitrary"),
                                             vmem_limit_bytes=VMEM_LIMIT),
        name="mlstm",
    )(p_m, p_g, g_row, conv_w, conv_b, gb_col, gb_row, mhn_w, tri, tri.T, rep, seg)


def _ffn_kernel(x_ref, yr_ref, ym_ref, mod_ref, wout_ref, n2g_ref, wup_ref, cw_ref, cb_ref,
                wdn_ref, nfg_ref, o_ref, tail, act_s):
    TM = TM_FFN
    t_idx = pl.program_id(1)

    @pl.when(t_idx == 0)
    def _():
        tail[...] = jnp.zeros(tail.shape, F32)

    y = (jnp.dot(yr_ref[...], wout_ref[0:D_RWKV, :], preferred_element_type=F32)
         + jnp.dot(ym_ref[...], wout_ref[D_RWKV:2 * D_RWKV, :], preferred_element_type=F32))
    x1 = x_ref[...] + mod_ref[2:3, :] * y
    h = _rms_mod(x1, n2g_ref[...], mod_ref[3:4, :], mod_ref[4:5, :]).astype(BF16)

    def conv(u, col0):
        cols = slice(col0, col0 + FF_TILE)
        prev_tail = tail[:, cols]
        tail[:, cols] = u[TM - SUBLANES:TM, :]
        out = cb_ref[:, cols] + cw_ref[FF_CONV - 1:FF_CONV, cols] * u
        for j in range(FF_CONV - 1):
            out = out + cw_ref[j:j + 1, cols] * _shift_rows(u, prev_tail, FF_CONV - 1 - j)
        return out

    for j in range(D_FF // FF_TILE):
        ca = j * FF_TILE
        cl = D_FF + j * FF_TILE
        ua = jnp.dot(h, wup_ref[:, ca:ca + FF_TILE], preferred_element_type=F32)
        ul = jnp.dot(h, wup_ref[:, cl:cl + FF_TILE], preferred_element_type=F32)
        act_s[:, ca:ca + FF_TILE] = (_silu(conv(ua, ca)) * conv(ul, cl)).astype(BF16)

    ffn = jnp.dot(act_s[...], wdn_ref[...], preferred_element_type=F32)
    x2 = x1 + mod_ref[5:6, :] * ffn
    o_ref[...] = (x2 * lax.rsqrt(jnp.mean(x2 * x2, axis=-1, keepdims=True) + NORM_EPS)
                  * nfg_ref[...])


def _ffn(x, y_r, y_m, mod, w_out, norm2_g, w_up, conv_w, conv_b, w_down, normf_g):
    B, T, _ = x.shape
    TM = TM_FFN
    blk = lambda n: pl.BlockSpec((None, TM, n), lambda b, t: (b, t, 0))
    return pl.pallas_call(
        _ffn_kernel,
        grid=(B, T // TM),
        in_specs=[blk(D_MODEL), blk(D_RWKV), blk(D_MLSTM),
                  pl.BlockSpec((None, 6, D_MODEL), lambda b, t: (b, 0, 0)),
                  _const_spec((D_MODEL, D_MODEL)), _const_spec((1, D_MODEL)),
                  _const_spec((D_MODEL, 2 * D_FF)), _const_spec((FF_CONV, 2 * D_FF)),
                  _const_spec((1, 2 * D_FF)), _const_spec((D_FF, D_MODEL)),
                  _const_spec((1, D_MODEL))],
        out_specs=blk(D_MODEL),
        out_shape=jax.ShapeDtypeStruct((B, T, D_MODEL), F32),
        scratch_shapes=[pltpu.VMEM((SUBLANES, 2 * D_FF), F32),
                        pltpu.VMEM((TM, D_FF), BF16)],
        compiler_params=pltpu.CompilerParams(dimension_semantics=("arbitrary", "arbitrary"),
                                             vmem_limit_bytes=VMEM_LIMIT),
        name="ffn",
    )(x, y_r, y_m, mod, w_out, norm2_g, w_up, conv_w, conv_b, w_down, normf_g)


def _pad_cols(a, n):
    return jnp.pad(a, ((0, 0), (0, n - a.shape[1])))


def _pad_rows(a, n):
    return jnp.pad(a, ((0, n - a.shape[0]), (0, 0)))


def kernel(x, c, w_ada, b_ada, norm1_g, norm2_g, normf_g, w_in, mu_rwkv, w_w2, w0, w_a2, a0, w_g2,
           k_k, k_a, r_k, lnx_w, lnx_b, conv_qk_w, conv_qk_b, i_bias, f_bias, mhn_w, w_out, w_up,
           conv_ff_w, conv_ff_b, w_down):
    assert w_ada.shape[0] == 1, "single-layer block"
    row = lambda a: a.reshape(1, -1)
    R3 = 3 * D_RWKV
    d_rin = R3 + W_LORA + A_LORA + G_LORA

    wi = w_in[0]

    def rwkv_cols(a):
        return jnp.concatenate([a[:, 0:R3], _pad_cols(a[:, R3:R3 + W_LORA], LANES),
                                _pad_cols(a[:, R3 + W_LORA:R3 + W_LORA + A_LORA], LANES),
                                a[:, R3 + W_LORA + A_LORA:d_rin]], axis=1)

    w_gate = wi[:, d_rin + D_M:]
    w_all = jnp.concatenate([rwkv_cols(wi[:, :d_rin]), _pad_cols(w_gate, LANES),
                             wi[:, d_rin:d_rin + D_M]], axis=1).astype(BF16)
    mu = rwkv_cols(row(mu_rwkv[0]))
    gates_b = jnp.concatenate([i_bias[0], f_bias[0]])
    gb_col = _pad_cols(row(gates_b), LANES)
    gb_row = jnp.broadcast_to(gates_b[:, None], (N_GATES, LANES))

    mod = _ada(c, w_ada, b_ada)
    mod = jnp.transpose(mod, (1, 0, 2))
    p_r, p_m, p_g, g_row = _inproj(x, mod, row(norm1_g[0]), w_all)
    y_r = _rwkv(p_r, mu, _pad_rows(w_w2[0], LANES).astype(BF16), row(w0[0]),
                _pad_rows(w_a2[0], LANES).astype(BF16), row(a0[0]), w_g2[0].astype(BF16),
                row(k_k[0]), row(k_a[0]), row(r_k[0]), row(lnx_w[0]), row(lnx_b[0]))
    y_m = _mlstm(p_m, p_g, g_row, conv_qk_w[0], row(conv_qk_b[0]), gb_col, gb_row, row(mhn_w[0]))
    return _ffn(x, y_r, y_m, mod, w_out[0].astype(BF16), row(norm2_g[0]), w_up[0].astype(BF16),
                conv_ff_w[0], row(conv_ff_b[0]), w_down[0].astype(BF16), row(normf_g))
```

```python
import jax
import jax.numpy as jnp
from jax import lax
from jax.experimental import pallas as pl
from jax.experimental.pallas import tpu as pltpu

D_MODEL = 1024
D_RWKV = 512
RWKV_HEAD = 64
RWKV_HEADS = 8
RWKV_PAIRS = RWKV_HEADS // 2
W_LORA = 64
A_LORA = 64
G_LORA = 128
D_MLSTM = 512
MLSTM_HEADS = 4
MLSTM_HEAD = 128
QK_CONV = 4
MLSTM_CHUNK = 64
D_FF = 2816
FF_CONV = 3
NORM_EPS = 1e-6
LNX_EPS = 64e-5
MHN_EPS = 1e-6

LANES = 128
SUBLANES = 8
RWKV_CHUNK = 64
INV_BASE = 2
SEG_TILE = 256
RWKV_SUB = 256
D_R = 3 * D_RWKV + 3 * LANES
D_M = 4 * D_MLSTM
D_P = D_R + D_M + LANES
N_GATES = 2 * MLSTM_HEADS

TM_IN = 1024
IN_SUB = 512
IN_COLS = 512
TB_RWKV = 512
TB_MLSTM = 1024
MLSTM_SUB = 256
TM_FFN = 1024
FF_TILE = 256
VMEM_LIMIT = 56 * 1024 * 1024

F32 = jnp.float32
BF16 = jnp.bfloat16


def _dot(a, b):
    return jnp.dot(a.astype(BF16), b.astype(BF16), preferred_element_type=F32)


def _dot_nt(a, b):
    return lax.dot_general(a.astype(BF16), b.astype(BF16), (((1,), (1,)), ((), ())),
                           preferred_element_type=F32)


def _dot_tn(a, b):
    return lax.dot_general(a.astype(BF16), b.astype(BF16), (((0,), (0,)), ((), ())),
                           preferred_element_type=F32)


def _head_sums(x, seg):
    hi = x.astype(BF16)
    lo = (x - hi.astype(F32)).astype(BF16)
    d = lambda t: jnp.dot(t, seg, preferred_element_type=F32)
    tiles = [d(hi[:, c:c + SEG_TILE]) + d(lo[:, c:c + SEG_TILE])
             for c in range(0, x.shape[1], SEG_TILE)]
    return jnp.concatenate(tiles, axis=1)


def _split3(x):
    hi = x.astype(BF16)
    r1 = x - hi.astype(F32)
    mid = r1.astype(BF16)
    lo = (r1 - mid.astype(F32)).astype(BF16)
    return hi, mid, lo


def _dot_sel_l(sel, x):
    hi, mid, lo = _split3(x)
    d = lambda t: jnp.dot(sel, t, preferred_element_type=F32)
    return d(hi) + (d(mid) + d(lo))


def _dot_sel_r(x, sel):
    hi, mid, lo = _split3(x)
    d = lambda t: jnp.dot(t, sel, preferred_element_type=F32)
    return d(hi) + (d(mid) + d(lo))


def _log_sigmoid(z):
    return jnp.minimum(z, 0.0) - jnp.log(1.0 + jnp.exp(-jnp.abs(z)))


def _silu(z):
    return z * jax.nn.sigmoid(z)


def _shift_rows(u, tail, d):
    rolled = pltpu.roll(u, d, axis=0)
    sub = lax.broadcasted_iota(jnp.int32, (SUBLANES, u.shape[1]), 0)
    head = jnp.where(sub < d, pltpu.roll(tail, d, axis=0), rolled[0:SUBLANES])
    return jnp.concatenate([head, rolled[SUBLANES:]], axis=0)


def _drain(gen):
    for _ in gen:
        pass


def _interleave(main, fillers):
    for _ in main:
        for f in fillers:
            next(f, None)
    for f in fillers:
        _drain(f)


def _const_spec(shape):
    nd = len(shape)
    return pl.BlockSpec(shape, lambda *_: (0,) * nd, pipeline_mode=pl.Buffered(1))


def _ada_kernel(c_ref, w_ref, b_ref, o_ref):
    s = _silu(c_ref[...])
    o_ref[0] = jnp.dot(s, w_ref[...], preferred_element_type=F32,
                       precision=lax.Precision.HIGHEST) + b_ref[...]


def _ada(c, w_ada, b_ada):
    B = c.shape[0]
    return pl.pallas_call(
        _ada_kernel,
        grid=(6,),
        in_specs=[pl.BlockSpec((B, D_MODEL), lambda j: (0, 0)),
                  pl.BlockSpec((None, D_MODEL, D_MODEL), lambda j: (0, 0, j)),
                  pl.BlockSpec((1, D_MODEL), lambda j: (0, j))],
        out_specs=pl.BlockSpec((1, B, D_MODEL), lambda j: (j, 0, 0)),
        out_shape=jax.ShapeDtypeStruct((6, B, D_MODEL), F32),
        compiler_params=pltpu.CompilerParams(dimension_semantics=("arbitrary",),
                                             vmem_limit_bytes=VMEM_LIMIT),
        name="ada",
    )(c, w_ada, b_ada)


def _rms_mod(x, g, shift, scale):
    y = x * lax.rsqrt(jnp.mean(x * x, axis=-1, keepdims=True) + NORM_EPS) * g
    return y * (1.0 + scale) + shift


def _inproj_kernel(x_ref, mod_ref, g_ref, w_ref, pr_ref, pm_ref, pg_ref, grow_ref):
    normed = {}

    def prologue(r0):
        rows = slice(r0, r0 + IN_SUB)
        normed[r0] = _rms_mod(x_ref[rows, :], g_ref[...], mod_ref[0:1, :],
                              mod_ref[1:2, :]).astype(BF16)
        yield

    def project(r0):
        rows = slice(r0, r0 + IN_SUB)
        h = normed.pop(r0)
        for c0 in range(0, D_R + LANES, IN_COLS):
            res = jnp.dot(h, w_ref[:, c0:c0 + IN_COLS], preferred_element_type=F32)
            if c0 + IN_COLS <= D_R:
                pr_ref[rows, c0:c0 + IN_COLS] = res
            else:
                pr_ref[rows, c0:D_R] = res[:, 0:D_R - c0]
                pg = res[:, D_R - c0:IN_COLS]
                pg_ref[rows, :] = pg
                grow_ref[:, rows] = pg.T[0:N_GATES, :]
            yield
        for c0 in range(0, D_M, IN_COLS):
            w_c0 = D_R + LANES + c0
            pm_ref[rows, c0:c0 + IN_COLS] = jnp.dot(h, w_ref[:, w_c0:w_c0 + IN_COLS],
                                                    preferred_element_type=F32)
            yield

    starts = list(range(0, TM_IN, IN_SUB))
    pros = {r0: prologue(r0) for r0 in starts}
    _drain(pros[starts[0]])
    for i, r0 in enumerate(starts):
        _interleave(project(r0), [pros[starts[i + 1]]] if i + 1 < len(starts) else [])


def _inproj(x, mod, norm_g, w_all):
    B, T, _ = x.shape
    return pl.pallas_call(
        _inproj_kernel,
        grid=(B, T // TM_IN),
        in_specs=[pl.BlockSpec((None, TM_IN, D_MODEL), lambda b, t: (b, t, 0)),
                  pl.BlockSpec((None, 6, D_MODEL), lambda b, t: (b, 0, 0)),
                  _const_spec((1, D_MODEL)),
                  _const_spec((D_MODEL, D_P))],
        out_specs=[pl.BlockSpec((None, TM_IN, D_R), lambda b, t: (b, t, 0)),
                   pl.BlockSpec((None, TM_IN, D_M), lambda b, t: (b, t, 0)),
                   pl.BlockSpec((None, TM_IN, LANES), lambda b, t: (b, t, 0)),
                   pl.BlockSpec((None, N_GATES, TM_IN), lambda b, t: (b, 0, t))],
        out_shape=[jax.ShapeDtypeStruct((B, T, D_R), F32),
                   jax.ShapeDtypeStruct((B, T, D_M), F32),
                   jax.ShapeDtypeStruct((B, T, LANES), F32),
                   jax.ShapeDtypeStruct((B, N_GATES, T), F32)],
        compiler_params=pltpu.CompilerParams(dimension_semantics=("arbitrary", "arbitrary"),
                                             vmem_limit_bytes=VMEM_LIMIT),
        name="inproj",
    )(x, mod, norm_g, w_all)


def _rwkv_kernel(p_ref, mu_ref, ww2_ref, w0_ref, wa2_ref, a0_ref, wg2_ref, kk_ref, ka_ref, rk_ref,
                 lnw_ref, lnb_ref, seg_ref, tri_ref, o_ref,
                 tail, rt_s, at_s, bt_s, kt_s, bh_s, kh_s, v_s, pt_s, state):
    TB, L, SUB = TB_RWKV, RWKV_CHUNK, RWKV_SUB
    t_idx = pl.program_id(1)

    @pl.when(t_idx == 0)
    def _():
        tail[...] = jnp.zeros(tail.shape, F32)
        state[...] = jnp.zeros(state.shape, F32)

    lane = lax.broadcasted_iota(jnp.int32, (1, LANES), 1)
    m0 = (lane < RWKV_HEAD).astype(F32)
    m1 = 1.0 - m0
    m0h, m1h = m0.astype(BF16), m1.astype(BF16)
    ri = lax.broadcasted_iota(jnp.int32, (2 * L, 2 * L), 0)
    ci = lax.broadcasted_iota(jnp.int32, (2 * L, 2 * L), 1)
    same = (ri // L) == (ci // L)
    tril_s = (same & (ri > ci)).astype(F32)
    tril_i = (same & (ri >= ci)).astype(F32)
    tril_sh, tril_ih = tril_s.astype(BF16), tril_i.astype(BF16)
    eye = (ri == ci).astype(F32)
    blockdiag = ((ri // RWKV_HEAD) == (ci // RWKV_HEAD)).astype(F32)
    base_mask = ((ri // INV_BASE) == (ci // INV_BASE)).astype(F32)
    level_masks = []
    size = INV_BASE
    while size < L:
        level_masks.append((((ri // (2 * size)) == (ci // (2 * size)))
                            & ((ri // size) != (ci // size))).astype(BF16))
        size *= 2

    def stack(t):
        return jnp.concatenate([t * m0, t * m1], axis=0)

    def stack_h(t):
        t = t.astype(BF16)
        return jnp.concatenate([t * m0h, t * m1h], axis=0)

    def chunk_tables(unit):
        c, pr = unit
        rows = slice(c * L, (c + 1) * L)
        cols = slice(pr * LANES, (pr + 1) * LANES)
        return dict(a=stack_h(at_s[rows, cols]), r=stack(rt_s[rows, cols]),
                    b=stack_h(bt_s[rows, cols]), k=stack_h(kt_s[rows, cols]),
                    v=stack_h(v_s[rows, cols]), bh=stack_h(bh_s[rows, cols]),
                    kh=stack_h(kh_s[rows, cols]), ptot=pt_s[c * L:c * L + 1, cols])

    handoff = {}

    def prologue(r0):
        rows = slice(r0, r0 + SUB)
        p = p_ref[rows, :]
        prev_tail = tail[...] if r0 == 0 else p_ref[r0 - SUBLANES:r0, :]
        xs = p + (_shift_rows(p, prev_tail, 1) - p) * mu_ref[...]
        r = xs[:, 0:D_RWKV]
        k = xs[:, D_RWKV:2 * D_RWKV]
        v = xs[:, 2 * D_RWKV:3 * D_RWKV]
        xw = xs[:, 3 * D_RWKV:3 * D_RWKV + LANES]
        xa = xs[:, 3 * D_RWKV + LANES:3 * D_RWKV + 2 * LANES]
        xg = xs[:, 3 * D_RWKV + 2 * LANES:D_R]
        yield
        w = _log_sigmoid(w0_ref[...] + _dot(jnp.tanh(xw), ww2_ref[...])) - 0.5
        yield
        a = jax.nn.sigmoid(a0_ref[...] + _dot(xa, wa2_ref[...]))
        g = _dot(jax.nn.sigmoid(xg), wg2_ref[...])
        yield
        seg = seg_ref[...]
        kk = k * kk_ref[...]
        kk = kk * lax.rsqrt(jnp.maximum(_head_sums(kk * kk, seg), 1e-24))
        yield
        k2 = k * (1.0 + (a - 1.0) * ka_ref[...])
        bonus = _head_sums(r * k2 * rk_ref[...], seg)
        handoff[r0] = dict(v=v, bonus=bonus, g=g)
        yield
        lw = -jnp.exp(w)
        cum = _dot_sel_l(tri_ref[...], lw)
        tot = jnp.concatenate([jnp.broadcast_to(cum[c * L + L - 1:c * L + L, :], (L, D_RWKV))
                               for c in range(SUB // L)], axis=0)
        yield
        b = kk * a
        rt_s[rows, :] = r * jnp.exp(cum)
        at_s[rows, :] = -kk * jnp.exp(cum - lw)
        yield
        e_neg = jnp.exp(-cum)
        bt_s[rows, :] = b * e_neg
        kt_s[rows, :] = k2 * e_neg
        yield
        e_end = jnp.exp(tot - cum)
        bh_s[rows, :] = b * e_end
        kh_s[rows, :] = k2 * e_end
        yield
        v_s[rows, :] = v
        pt_s[rows, :] = jnp.exp(tot)

    def algebra(r0):
        chunks = range(r0 // L, (r0 + SUB) // L)
        units = [(c, pr) for c in chunks for pr in range(RWKV_PAIRS)]
        tabs = [chunk_tables(u) for u in units]
        yield
        prods = [_dot_nt(jnp.concatenate([t["a"], t["r"].astype(BF16)], axis=0),
                         jnp.concatenate([t["b"], t["k"]], axis=0)) for t in tabs]
        yield
        n_ab = [x[0:2 * L, 0:2 * L] * tril_s for x in prods]
        a_ak = [x[0:2 * L, 2 * L:4 * L].astype(BF16) * tril_sh for x in prods]
        a_rb = [x[2 * L:4 * L, 0:2 * L].astype(BF16) * tril_ih for x in prods]
        a_rk = [x[2 * L:4 * L, 2 * L:4 * L].astype(BF16) * tril_ih for x in prods]
        n_abh = [n.astype(BF16) for n in n_ab]
        inv = [eye + n * base_mask for n in n_ab]
        for lvl_mask in level_masks:
            invh = [m.astype(BF16) for m in inv]
            cm = [_dot(n * lvl_mask, mh) for n, mh in zip(n_abh, invh)]
            yield
            inv = [m + _dot(mh, x) for m, mh, x in zip(inv, invh, cm)]
            yield
        av = [_dot(jnp.concatenate([x, y], axis=0), t["v"]) for x, y, t in zip(a_ak, a_rk, tabs)]
        yield
        wm = [_dot(m, jnp.concatenate([t["a"], x[0:2 * L].astype(BF16)], axis=1))
              for m, t, x in zip(inv, tabs, av)]
        yield
        q = [_dot(x, y) for x, y in zip(a_rb, wm)]
        yield
        tn = [_dot_tn(t["bh"], y) for t, y in zip(tabs, wm)]
        yield
        kv = [_dot_tn(t["kh"], t["v"]) for t in tabs]
        r_p = [t["r"] + x[:, 0:LANES] for t, x in zip(tabs, q)]
        y_pre = [x[:, LANES:2 * LANES] + y[2 * L:4 * L] for x, y in zip(q, av)]
        gmat = [eye * t["ptot"] + x[:, 0:LANES] for t, x in zip(tabs, tn)]
        t_pre = [x[:, LANES:2 * LANES] + y for x, y in zip(tn, kv)]
        yield
        y_chunks = []
        for j in range(len(chunks)):
            us = [j * RWKV_PAIRS + pr for pr in range(RWKV_PAIRS)]
            sts = [state[pr] for pr in range(RWKV_PAIRS)]
            both = [_dot(jnp.concatenate([r_p[u], gmat[u]], axis=0), st) for u, st in zip(us, sts)]
            y_pairs = []
            for pr, (u, z) in enumerate(zip(us, both)):
                y_st = z[0:2 * L] + y_pre[u]
                state[pr] = (z[2 * L:4 * L] + t_pre[u]) * blockdiag
                y_pairs.append(y_st[0:L, :] + y_st[L:2 * L, :])
            y_chunks.append(jnp.concatenate(y_pairs, axis=1))
            yield
        handoff[r0]["y"] = jnp.concatenate(y_chunks, axis=0)

    def epilogue(r0):
        h = handoff.pop(r0)
        seg = seg_ref[...]
        y = h["y"]
        mean = _head_sums(y, seg) * (1.0 / RWKV_HEAD)
        yield
        yc = y - mean
        var = _head_sums(yc * yc, seg) * (1.0 / RWKV_HEAD)
        yield
        yn = yc * lax.rsqrt(var + LNX_EPS) * lnw_ref[...] + lnb_ref[...]
        o_ref[r0:r0 + SUB, :] = ((yn + h["bonus"] * h["v"]) * h["g"]).astype(o_ref.dtype)

    starts = list(range(0, TB, SUB))
    pros = {r0: prologue(r0) for r0 in starts}
    _drain(pros[starts[0]])
    for i, r0 in enumerate(starts):
        fillers = []
        if i + 1 < len(starts):
            fillers.append(pros[starts[i + 1]])
        if i >= 1:
            fillers.append(epilogue(starts[i - 1]))
        _interleave(algebra(r0), fillers)
    _drain(epilogue(starts[-1]))
    tail[...] = p_ref[TB - SUBLANES:TB, :]


def _rwkv(p_r, mu, ww2, w0, wa2, a0, wg2, k_k, k_a, r_k, lnx_w, lnx_b):
    B, T, _ = p_r.shape
    TB, L = TB_RWKV, RWKV_CHUNK
    iseg = jnp.arange(SEG_TILE)
    seg = (iseg[:, None] // RWKV_HEAD == iseg[None, :] // RWKV_HEAD).astype(BF16)
    it = jnp.arange(RWKV_SUB)
    same_chunk = it[:, None] // L == it[None, :] // L
    tri = (same_chunk & (it[:, None] >= it[None, :])).astype(BF16)
    row = lambda n: _const_spec((1, n))
    blk = lambda n: pl.BlockSpec((None, TB, n), lambda b, t: (b, t, 0))
    scr = lambda: pltpu.VMEM((TB, D_RWKV), F32)
    return pl.pallas_call(
        _rwkv_kernel,
        grid=(B, T // TB),
        in_specs=[blk(D_R), row(D_R), _const_spec((LANES, D_RWKV)), row(D_RWKV),
                  _const_spec((LANES, D_RWKV)), row(D_RWKV), _const_spec((G_LORA, D_RWKV)),
                  row(D_RWKV), row(D_RWKV), row(D_RWKV), row(D_RWKV), row(D_RWKV),
                  _const_spec((SEG_TILE, SEG_TILE)), _const_spec((RWKV_SUB, RWKV_SUB))],
        out_specs=blk(D_RWKV),
        out_shape=jax.ShapeDtypeStruct((B, T, D_RWKV), BF16),
        scratch_shapes=[pltpu.VMEM((SUBLANES, D_R), F32)] + [scr() for _ in range(8)]
                       + [pltpu.VMEM((RWKV_PAIRS, LANES, LANES), F32)],
        compiler_params=pltpu.CompilerParams(dimension_semantics=("arbitrary", "arbitrary"),
                                             vmem_limit_bytes=VMEM_LIMIT),
        name="rwkv",
    )(p_r, mu, ww2, w0, wa2, a0, wg2, k_k, k_a, r_k, lnx_w, lnx_b, seg, tri)


def _chunk_cummax(x, pos):
    d = 1
    while d < MLSTM_CHUNK:
        x = jnp.maximum(x, jnp.where(pos >= d, pltpu.roll(x, d, axis=0), -jnp.inf))
        d *= 2
    return x


def _mlstm_kernel(pm_ref, pg_ref, grow_ref, cw_ref, cb_ref, gbc_ref, gbr_ref, mhn_ref, tri_ref,
                  trit_ref, rep_ref, seg_ref, o_ref, tail, q_s, k_s, cn_state, m_state):
    TB, SUB, L, H, N = TB_MLSTM, MLSTM_SUB, MLSTM_CHUNK, MLSTM_HEADS, MLSTM_HEAD
    DQK = 2 * D_MLSTM
    NC = SUB // L
    t_idx = pl.program_id(1)

    @pl.when(t_idx == 0)
    def _():
        tail[...] = jnp.zeros(tail.shape, F32)
        cn_state[...] = jnp.zeros(cn_state.shape, F32)
        m_state[...] = jnp.zeros(m_state.shape, F32)

    lane = lax.broadcasted_iota(jnp.int32, (1, LANES), 1)
    sub = lax.broadcasted_iota(jnp.int32, (N_GATES, 1), 0)
    pos = lax.broadcasted_iota(jnp.int32, (SUB, LANES), 0) % L
    ti = lax.broadcasted_iota(jnp.int32, (L, L), 0)
    si = lax.broadcasted_iota(jnp.int32, (L, L), 1)
    causal = ti >= si
    ones_v = jnp.ones((L, N), F32)
    handoff = {}

    def prologue(r0):
        rows = slice(r0, r0 + SUB)
        gc = pg_ref[rows, :] + gbc_ref[...]
        gc = jnp.where(lane < H, gc, jnp.where(lane < 2 * H, _log_sigmoid(gc), 0.0))
        rep = _dot_sel_r(gc, rep_ref[...])
        yield
        ig_c = rep[:, 0:H * LANES]
        b_c = _dot_sel_l(tri_ref[...], rep[:, H * LANES:2 * H * LANES])
        g_c = ig_c - b_c
        yield
        gr = grow_ref[:, rows] + gbr_ref[:, 0:1]
        gr = jnp.where(sub < H, gr, _log_sigmoid(gr))
        g_r = gr[0:H, :] - _dot_sel_r(gr, trit_ref[...])[H:2 * H, :]
        yield
        x = pm_ref[rows, 0:DQK]
        prev_tail = tail[...] if r0 == 0 else pm_ref[r0 - SUBLANES:r0, 0:DQK]
        acc = cb_ref[...] + cw_ref[QK_CONV - 1:QK_CONV, :] * x
        for j in range(QK_CONV - 1):
            acc = acc + cw_ref[j:j + 1, :] * _shift_rows(x, prev_tail, QK_CONV - 1 - j)
            yield
        qk = _silu(acc)
        q_s[rows, :] = qk[:, 0:D_MLSTM]
        k_s[rows, :] = qk[:, D_MLSTM:DQK] * (N ** -0.5)
        yield
        a_in, m_in = {}, {}
        for h in range(H):
            cols = slice(h * LANES, (h + 1) * LANES)
            cmax = _chunk_cummax(g_c[:, cols], pos)
            m = m_state[h]
            for c in range(NC):
                a = jnp.maximum(m, cmax[c * L:(c + 1) * L, :])
                a_in[(c, h)], m_in[(c, h)] = a, m
                m = b_c[c * L + L - 1:c * L + L, cols] + a[L - 1:L, :]
            m_state[h] = m
            yield
        handoff[r0] = dict(a_in=a_in, m_in=m_in, g_c=g_c, b_c=b_c, g_r=g_r)

    def mix(r0):
        hd = handoff[r0]
        a_in, m_in, g_c, b_c, g_r = hd["a_in"], hd["m_in"], hd["g_c"], hd["b_c"], hd["g_r"]
        units = [(c, h) for c in range(NC) for h in range(H)]

        def load(c, h):
            rows = slice(r0 + c * L, r0 + (c + 1) * L)
            cols = slice(h * N, (h + 1) * N)
            return (q_s[rows, cols], k_s[rows, cols],
                    pm_ref[rows, 2 * D_MLSTM + h * N:2 * D_MLSTM + (h + 1) * N])

        qkv = {u: load(*u) for u in units}
        dw = {(c, h): jnp.exp(jnp.where(causal, g_r[h:h + 1, c * L:(c + 1) * L]
                                        - a_in[(c, h)][:, 0:L], -jnp.inf)) for c, h in units}
        yield
        iw = {u: jnp.exp(m_in[u] - a_in[u]) for u in units}
        s = {u: _dot_nt(qkv[u][0], qkv[u][1]) * dw[u] for u in units}
        yield
        sv = {u: _dot(s[u], jnp.concatenate([qkv[u][2], ones_v], axis=1)) for u in units}
        yield
        a_end = {u: a_in[u][L - 1:L, :] for u in units}
        ws = {(c, h): jnp.exp(g_c[c * L:(c + 1) * L, h * LANES:(h + 1) * LANES] - a_end[(c, h)])
              for c, h in units}
        dec = {u: jnp.exp(m_in[u] - a_end[u]) for u in units}
        yield
        kv = {u: _dot_tn(qkv[u][1], jnp.concatenate([ws[u] * qkv[u][2], ws[u]], axis=1))
              for u in units}
        yield
        cn_in = {}
        for h in range(H):
            cn = cn_state[h]
            for c in range(NC):
                cn_in[(c, h)] = cn
                cn = jnp.concatenate([dec[(c, h)], dec[(c, h)]], axis=1) * cn + kv[(c, h)]
            cn_state[h] = cn
        yield
        qc = {u: _dot(qkv[u][0], cn_in[u]) for u in units}
        yield
        hs = {}
        for c, h in units:
            u = (c, h)
            num = sv[u][:, 0:N] + iw[u] * qc[u][:, 0:N]
            den = sv[u][:, N:2 * N] + iw[u] * qc[u][:, N:2 * N]
            m_t = b_c[c * L:(c + 1) * L, h * LANES:(h + 1) * LANES] + a_in[u]
            hs[u] = num / jnp.maximum(jnp.abs(den), jnp.exp(-m_t))
            if h == H - 1:
                yield
        hd["hh"] = jnp.concatenate([jnp.concatenate([hs[(c, h)] for h in range(H)], axis=1)
                                    for c in range(NC)], axis=0)

    def epilogue(r0):
        rows = slice(r0, r0 + SUB)
        hh = handoff.pop(r0)["hh"]
        seg = seg_ref[...]
        mean = _head_sums(hh, seg) * (1.0 / N)
        yield
        yc = hh - mean
        var = _head_sums(yc * yc, seg) * (1.0 / N)
        yield
        yn = yc * lax.rsqrt(var + MHN_EPS) * mhn_ref[...]
        o_ref[rows, :] = (yn * jax.nn.sigmoid(pm_ref[rows, 3 * D_MLSTM:4 * D_MLSTM])).astype(o_ref.dtype)

    starts = list(range(0, TB, SUB))
    pros = {r0: prologue(r0) for r0 in starts}
    _drain(pros[starts[0]])
    for i, r0 in enumerate(starts):
        fillers = []
        if i + 1 < len(starts):
            fillers.append(pros[starts[i + 1]])
        if i >= 1:
            fillers.append(epilogue(starts[i - 1]))
        _interleave(mix(r0), fillers)
    _drain(epilogue(starts[-1]))
    tail[...] = pm_ref[TB - SUBLANES:TB, 0:DQK]


def _mlstm(p_m, p_g, g_row, conv_w, conv_b, gb_col, gb_row, mhn_w):
    B, T, _ = p_m.shape
    TB, SUB, L = TB_MLSTM, MLSTM_SUB, MLSTM_CHUNK
    it = jnp.arange(SUB)
    same_chunk = it[:, None] // L == it[None, :] // L
    tri = (same_chunk & (it[:, None] >= it[None, :])).astype(BF16)
    rep = (jnp.arange(LANES)[:, None] == jnp.arange(N_GATES * LANES)[None, :] // LANES).astype(BF16)
    iseg = jnp.arange(SEG_TILE)
    seg = (iseg[:, None] // MLSTM_HEAD == iseg[None, :] // MLSTM_HEAD).astype(BF16)
    blk = lambda n: pl.BlockSpec((None, TB, n), lambda b, t: (b, t, 0))
    return pl.pallas_call(
        _mlstm_kernel,
        grid=(B, T // TB),
        in_specs=[blk(D_M), blk(LANES),
                  pl.BlockSpec((None, N_GATES, TB), lambda b, t: (b, 0, t)),
                  _const_spec((QK_CONV, 2 * D_MLSTM)), _const_spec((1, 2 * D_MLSTM)),
                  _const_spec((1, LANES)), _const_spec((N_GATES, LANES)),
                  _const_spec((1, D_MLSTM)), _const_spec((SUB, SUB)), _const_spec((SUB, SUB)),
                  _const_spec((LANES, N_GATES * LANES)), _const_spec((SEG_TILE, SEG_TILE))],
        out_specs=blk(D_MLSTM),
        out_shape=jax.ShapeDtypeStruct((B, T, D_MLSTM), BF16),
        scratch_shapes=[pltpu.VMEM((SUBLANES, 2 * D_MLSTM), F32),
                        pltpu.VMEM((TB, D_MLSTM), F32), pltpu.VMEM((TB, D_MLSTM), F32),
                        pltpu.VMEM((MLSTM_HEADS, MLSTM_HEAD, 2 * MLSTM_HEAD), F32),
                        pltpu.VMEM((MLSTM_HEADS, 1, LANES), F32)],
        compiler_params=pltpu.CompilerParams(dimension_semantics=("arbitrary", "arbitrary"),
                                             vmem_limit_bytes=VMEM_LIMIT),
        name="mlstm",
    )(p_m, p_g, g_row, conv_w, conv_b, gb_col, gb_row, mhn_w, tri, tri.T, rep, seg)


def _ffn_kernel(x_ref, yr_ref, ym_ref, mod_ref, wout_ref, n2g_ref, wup_ref, cw_ref, cb_ref,
                wdn_ref, nfg_ref, o_ref, tail, act_s):
    TM = TM_FFN
    t_idx = pl.program_id(1)

    @pl.when(t_idx == 0)
    def _():
        tail[...] = jnp.zeros(tail.shape, F32)

    y = (jnp.dot(yr_ref[...], wout_ref[0:D_RWKV, :], preferred_element_type=F32)
         + jnp.dot(ym_ref[...], wout_ref[D_RWKV:2 * D_RWKV, :], preferred_element_type=F32))
    x1 = x_ref[...] + mod_ref[2:3, :] * y
    h = _rms_mod(x1, n2g_ref[...], mod_ref[3:4, :], mod_ref[4:5, :]).astype(BF16)

    def conv(u, col0):
        cols = slice(col0, col0 + FF_TILE)
        prev_tail = tail[:, cols]
        tail[:, cols] = u[TM - SUBLANES:TM, :]
        out = cb_ref[:, cols] + cw_ref[FF_CONV - 1:FF_CONV, cols] * u
        for j in range(FF_CONV - 1):
            out = out + cw_ref[j:j + 1, cols] * _shift_rows(u, prev_tail, FF_CONV - 1 - j)
        return out

    for j in range(D_FF // FF_TILE):
        ca = j * FF_TILE
        cl = D_FF + j * FF_TILE
        ua = jnp.dot(h, wup_ref[:, ca:ca + FF_TILE], preferred_element_type=F32)
        ul = jnp.dot(h, wup_ref[:, cl:cl + FF_TILE], preferred_element_type=F32)
        act_s[:, ca:ca + FF_TILE] = (_silu(conv(ua, ca)) * conv(ul, cl)).astype(BF16)

    ffn = jnp.dot(act_s[...], wdn_ref[...], preferred_element_type=F32)
    x2 = x1 + mod_ref[5:6, :] * ffn
    o_ref[...] = (x2 * lax.rsqrt(jnp.mean(x2 * x2, axis=-1, keepdims=True) + NORM_EPS)
                  * nfg_ref[...])


def _ffn(x, y_r, y_m, mod, w_out, norm2_g, w_up, conv_w, conv_b, w_down, normf_g):
    B, T, _ = x.shape
    TM = TM_FFN
    blk = lambda n: pl.BlockSpec((None, TM, n), lambda b, t: (b, t, 0))
    return pl.pallas_call(
        _ffn_kernel,
        grid=(B, T // TM),
        in_specs=[blk(D_MODEL), blk(D_RWKV), blk(D_MLSTM),
                  pl.BlockSpec((None, 6, D_MODEL), lambda b, t: (b, 0, 0)),
                  _const_spec((D_MODEL, D_MODEL)), _const_spec((1, D_MODEL)),
                  _const_spec((D_MODEL, 2 * D_FF)), _const_spec((FF_CONV, 2 * D_FF)),
                  _const_spec((1, 2 * D_FF)), _const_spec((D_FF, D_MODEL)),
                  _const_spec((1, D_MODEL))],
        out_specs=blk(D_MODEL),
        out_shape=jax.ShapeDtypeStruct((B, T, D_MODEL), F32),
        scratch_shapes=[pltpu.VMEM((SUBLANES, 2 * D_FF), F32),
                        pltpu.VMEM((TM, D_FF), BF16)],
        compiler_params=pltpu.CompilerParams(dimension_semantics=("arbitrary", "arbitrary"),
                                             vmem_limit_bytes=VMEM_LIMIT),
        name="ffn",
    )(x, y_r, y_m, mod, w_out, norm2_g, w_up, conv_w, conv_b, w_down, normf_g)


def _pad_cols(a, n):
    return jnp.pad(a, ((0, 0), (0, n - a.shape[1])))


def _pad_rows(a, n):
    return jnp.pad(a, ((0, n - a.shape[0]), (0, 0)))


def kernel(x, c, w_ada, b_ada, norm1_g, norm2_g, normf_g, w_in, mu_rwkv, w_w2, w0, w_a2, a0, w_g2,
           k_k, k_a, r_k, lnx_w, lnx_b, conv_qk_w, conv_qk_b, i_bias, f_bias, mhn_w, w_out, w_up,
           conv_ff_w, conv_ff_b, w_down):
    assert w_ada.shape[0] == 1, "single-layer block"
    row = lambda a: a.reshape(1, -1)
    R3 = 3 * D_RWKV
    d_rin = R3 + W_LORA + A_LORA + G_LORA

    wi = w_in[0]

    def rwkv_cols(a):
        return jnp.concatenate([a[:, 0:R3], _pad_cols(a[:, R3:R3 + W_LORA], LANES),
                                _pad_cols(a[:, R3 + W_LORA:R3 + W_LORA + A_LORA], LANES),
                                a[:, R3 + W_LORA + A_LORA:d_rin]], axis=1)

    w_gate = wi[:, d_rin + D_M:]
    w_all = jnp.concatenate([rwkv_cols(wi[:, :d_rin]), _pad_cols(w_gate, LANES),
                             wi[:, d_rin:d_rin + D_M]], axis=1).astype(BF16)
    mu = rwkv_cols(row(mu_rwkv[0]))
    gates_b = jnp.concatenate([i_bias[0], f_bias[0]])
    gb_col = _pad_cols(row(gates_b), LANES)
    gb_row = jnp.broadcast_to(gates_b[:, None], (N_GATES, LANES))

    mod = _ada(c, w_ada, b_ada)
    mod = jnp.transpose(mod, (1, 0, 2))
    p_r, p_m, p_g, g_row = _inproj(x, mod, row(norm1_g[0]), w_all)
    y_r = _rwkv(p_r, mu, _pad_rows(w_w2[0], LANES).astype(BF16), row(w0[0]),
                _pad_rows(w_a2[0], LANES).astype(BF16), row(a0[0]), w_g2[0].astype(BF16),
                row(k_k[0]), row(k_a[0]), row(r_k[0]), row(lnx_w[0]), row(lnx_b[0]))
    y_m = _mlstm(p_m, p_g, g_row, conv_qk_w[0], row(conv_qk_b[0]), gb_col, gb_row, row(mhn_w[0]))
    return _ffn(x, y_r, y_m, mod, w_out[0].astype(BF16), row(norm2_g[0]), w_up[0].astype(BF16),
                conv_ff_w[0], row(conv_ff_b[0]), w_down[0].astype(BF16), row(normf_g))
```
